```python
import math
import jax
import jax.numpy as jnp
from jax import lax
import numpy as np

D_MODEL = 1024
BATCH = 4
SEQ = 8192
DEPTH = 4
DEC_BATCH = 16
DEC_SEQ = 32
PAST_LEN = 2048

CHUNK = 64
Q_BLOCK = 128
EPS = 1e-6
A_HEADS = 4
A_HEAD_DIM = 64
A_VDIM = 2 * A_HEAD_DIM
A_WIDTH = A_HEADS * A_VDIM
ROPE_DIM = A_HEAD_DIM // 4
ROPE_THETA = 500000.0
G_HEADS = 4
G_DK = 64
G_DV = 128
G_WIDTH = G_HEADS * G_DV
G_GATE_RANK = 16
G_GATE_TAU = 16.0
LRU_WIDTH = 512
LRU_BLOCKS = 8
LRU_BLOCK = LRU_WIDTH // LRU_BLOCKS
LRU_CONV = 4
LRU_C = 8.0
D_FF = 2816
FFN_CONV = 3
N_BRANCH = 3
PROJ_SPLITS = (2 * A_HEADS * A_HEAD_DIM, 2 * A_HEADS * A_HEAD_DIM, A_WIDTH, G_HEADS * G_DK, G_HEADS * G_DK, G_WIDTH, G_WIDTH, G_GATE_RANK, LRU_WIDTH, LRU_WIDTH)
PROJ_WIDTH = sum(PROJ_SPLITS)

kernel_name = 'hybrid_stream_encoder_step'


def rmsnorm(x, g):
    xf = x.astype(jnp.float32)
    y = xf * lax.rsqrt(jnp.mean(xf * xf, axis=-1, keepdims=True) + EPS) * g.astype(jnp.float32)
    return y.astype(x.dtype)


def head_rmsnorm(o, g):
    return o * lax.rsqrt(jnp.mean(o * o, axis=-1, keepdims=True) + EPS) * g.astype(jnp.float32)


def split_cols(z, widths):
    outs = []
    start = 0
    for w in widths:
        outs.append(z[..., start:start + w])
        start += w
    return outs


def causal_dwconv(x, buf, w, b):
    width = w.shape[0]
    t = x.shape[1]
    xp = jnp.concatenate([buf.astype(x.dtype), x], axis=1)
    y = b
    for j in range(width):
        y = y + xp[:, j:j + t] * w[j]
    return y, xp[:, t:]


def rope(x, pos):
    half = ROPE_DIM // 2
    inv = ROPE_THETA ** (-jnp.arange(half, dtype=jnp.float32) / half)
    ang = pos[:, None] * inv[None, :]
    cos = jnp.cos(ang)[:, None, None, :]
    sin = jnp.sin(ang)[:, None, None, :]
    x1 = x[..., :half]
    x2 = x[..., half:ROPE_DIM]
    return jnp.concatenate([x1 * cos - x2 * sin, x2 * cos + x1 * sin, x[..., ROPE_DIM:]], axis=-1)


def diff_scores(q, k, v, mask, lam):
    s = jnp.einsum('bqhcd,bkhcd->bhcqk', q, k) * (A_HEAD_DIM ** -0.5)
    s = jnp.where(mask, s, -jnp.inf)
    p = jax.nn.softmax(s, axis=-1)
    p = p[:, :, 0] - lam * p[:, :, 1]
    return jnp.einsum('bhqk,bkhe->bqhe', p, v)


def diff_attention(q, k, v, q_pos, k_pos, lam):
    b, t, h, _, d = q.shape
    qc = q_pos // CHUNK
    kc = k_pos // CHUNK
    if t > Q_BLOCK and t % Q_BLOCK == 0:
        nb = t // Q_BLOCK
        qb = q.reshape(b, nb, Q_BLOCK, h, 2, d).swapaxes(0, 1)
        qcb = qc.reshape(nb, Q_BLOCK)

        def block(args):
            qi, qci = args
            return diff_scores(qi, k, v, kc[None, :] <= qci[:, None], lam)

        o = lax.map(block, (qb, qcb))
        return o.swapaxes(0, 1).reshape(b, t, h, v.shape[-1])
    return diff_scores(q, k, v, kc[None, :] <= qc[:, None], lam)


def gla_chunk(s0, q, k, v, la):
    L = q.shape[1]
    b_cum = jnp.cumsum(la, axis=1)
    inter = jnp.einsum('bthk,bhkv->bthv', q * jnp.exp(b_cum), s0)
    tri = jnp.tril(jnp.ones((L, L), dtype=bool))
    dif = b_cum[:, :, None] - b_cum[:, None, :]
    decay = jnp.exp(jnp.where(tri[None, :, :, None, None], dif, -jnp.inf))
    att = jnp.einsum('bthk,btshk,bshk->bhts', q, decay, k)
    intra = jnp.einsum('bhts,bshv->bthv', att, v)
    b_last = b_cum[:, -1]
    s_new = jnp.exp(b_last)[..., None] * s0 + jnp.einsum('bshk,bshv->bhkv', k * jnp.exp(b_last[:, None] - b_cum), v)
    return s_new, inter + intra


def gla(q, k, v, la, s0):
    b, t, h, _ = q.shape
    L = min(CHUNK, t)
    nc = t // L

    def to_blocks(a):
        return a.reshape(b, nc, L, h, a.shape[-1]).swapaxes(0, 1)

    def step(s, inp):
        qc, kc, vc, lc = inp
        return gla_chunk(s, qc, kc, vc, lc)

    s_fin, o = lax.scan(step, s0, (to_blocks(q), to_blocks(k), to_blocks(v), to_blocks(la)))
    return s_fin, o.swapaxes(0, 1).reshape(b, t, h, v.shape[-1])


def _lin_combine(e1, e2):
    a1, b1 = e1
    a2, b2 = e2
    return a1 * a2, a2 * b1 + b2


def rg_lru(xc, h0, wa, ba, wx, bx, lam):
    b, t, w = xc.shape
    xf = xc.astype(jnp.float32)
    xb = xf.reshape(b, t, LRU_BLOCKS, LRU_BLOCK)
    r = jax.nn.sigmoid(jnp.einsum('btni,nij->btnj', xb, wa.astype(jnp.float32)).reshape(b, t, w) + ba.astype(jnp.float32))
    i = jax.nn.sigmoid(jnp.einsum('btni,nij->btnj', xb, wx.astype(jnp.float32)).reshape(b, t, w) + bx.astype(jnp.float32))
    log_a = -LRU_C * r * jax.nn.softplus(-lam.astype(jnp.float32))
    a = jnp.exp(log_a)
    u = jnp.sqrt(-jnp.expm1(2.0 * log_a)) * (i * xf)
    a_cum, h_zero = lax.associative_scan(_lin_combine, (a, u), axis=1)
    h = a_cum * h0[:, None, :] + h_zero
    return h, h[:, -1]


def layer(x, l, k_past, v_past, s_gla, lru_buf, h_lru, ffn_buf, params):
    (norm_mix, w_in, lambda_qk, attn_subln, w_gla_gate2, b_gla_gate, gla_norm, lru_conv_w, lru_conv_b,
     lru_wa, lru_ba, lru_wx, lru_bx, lru_lambda, w_branch_attn, w_branch_gla, w_branch_lru, w_merge, b_merge,
     w_out, norm_ffn, w_ffn_gate, ffn_conv_w, ffn_conv_b, w_ffn_up, w_ffn_down) = params
    f32 = jnp.float32
    dt = x.dtype
    b, t, _ = x.shape
    p_len = k_past.shape[1]
    xn = rmsnorm(x, norm_mix)
    z = xn @ w_in
    aq, ak, av, gq, gk, gv, gr, ga, lx, lg = split_cols(z, PROJ_SPLITS)

    q_pos = p_len + jnp.arange(t, dtype=jnp.int32)
    k_pos = jnp.arange(p_len + t, dtype=jnp.int32)
    rp = q_pos.astype(f32)
    q = rope(aq.astype(f32).reshape(b, t, A_HEADS, 2, A_HEAD_DIM), rp)
    k = rope(ak.astype(f32).reshape(b, t, A_HEADS, 2, A_HEAD_DIM), rp)
    v = av.astype(f32).reshape(b, t, A_HEADS, A_VDIM)
    k_all = jnp.concatenate([k_past.astype(f32), k], axis=1)
    v_all = jnp.concatenate([v_past.astype(f32), v], axis=1)
    lam_init = 0.8 - 0.6 * math.exp(-0.3 * l)
    lq = lambda_qk.astype(f32)
    lam = jnp.exp(jnp.sum(lq[0] * lq[1])) - jnp.exp(jnp.sum(lq[2] * lq[3])) + lam_init
    o = diff_attention(q, k_all, v_all, q_pos, k_pos, lam)
    o = head_rmsnorm(o, attn_subln) * (1.0 - lam_init)
    y_attn = o.reshape(b, t, A_WIDTH).astype(dt) @ w_branch_attn

    gq_ = gq.astype(f32).reshape(b, t, G_HEADS, G_DK) * (G_DK ** -0.5)
    gk_ = gk.astype(f32).reshape(b, t, G_HEADS, G_DK)
    gv_ = gv.astype(f32).reshape(b, t, G_HEADS, G_DV)
    la = jax.nn.log_sigmoid((ga @ w_gla_gate2 + b_gla_gate).astype(f32)).reshape(b, t, G_HEADS, G_DK) / G_GATE_TAU
    s_new, go = gla(gq_, gk_, gv_, la, s_gla.astype(f32))
    go = head_rmsnorm(go, gla_norm) * jax.nn.silu(gr.astype(f32).reshape(b, t, G_HEADS, G_DV))
    y_gla = go.reshape(b, t, G_WIDTH).astype(dt) @ w_branch_gla

    xc, lru_buf_new = causal_dwconv(lx, lru_buf, lru_conv_w, lru_conv_b)
    h, h_last = rg_lru(xc, h_lru.astype(f32), lru_wa, lru_ba, lru_wx, lru_bx, lru_lambda)
    y_lru = (h * jax.nn.gelu(lg.astype(f32))).astype(dt) @ w_branch_lru

    g = jax.nn.sigmoid((xn @ w_merge + b_merge).astype(f32)).reshape(b, t, N_BRANCH, D_MODEL)
    merged = g[:, :, 0] * y_attn.astype(f32) + g[:, :, 1] * y_gla.astype(f32) + g[:, :, 2] * y_lru.astype(f32)
    x = x + merged.astype(dt) @ w_out

    hn = rmsnorm(x, norm_ffn)
    gu = hn @ w_ffn_gate
    gc, ffn_buf_new = causal_dwconv(gu, ffn_buf, ffn_conv_w, ffn_conv_b)
    f = jax.nn.gelu(gc.astype(f32)) * (hn @ w_ffn_up).astype(f32)
    x = x + f.astype(dt) @ w_ffn_down
    return x, (k.astype(dt), v.astype(dt), s_new.astype(dt), lru_buf_new, h_last.astype(dt), ffn_buf_new)


def trunk(x, cache_k, cache_v, st_gla, st_lru_conv, st_lru_h, st_ffn_conv, layer_params, norm_final):
    outs = [[], [], [], [], [], []]
    for l in range(DEPTH):
        p = tuple(w[l] for w in layer_params)
        x, new = layer(x, l, cache_k[l], cache_v[l], st_gla[l], st_lru_conv[l], st_lru_h[l], st_ffn_conv[l], p)
        for lst, a in zip(outs, new):
            lst.append(a)
    y = rmsnorm(x, norm_final)
    return y, [jnp.stack(lst) for lst in outs]


def setup_inputs(seed: int = 0) -> dict:
    key = jax.random.key(seed)
    ks = iter(list(jax.random.split(key, 64)))
    f32 = jnp.float32

    def nrm(shape, scale):
        return jax.random.normal(next(ks), shape, f32) * scale

    def gain(shape):
        return 1.0 + nrm(shape, 0.02)

    u = jax.random.uniform(next(ks), (DEPTH, LRU_WIDTH), f32, 0.9, 0.999)
    base = u ** (1.0 / LRU_C)
    lru_lambda = jnp.log(base) - jnp.log1p(-base)
    return {
        'x_prompt': nrm((BATCH, SEQ, D_MODEL), 1.0),
        'x_sample': nrm((DEC_BATCH, DEC_SEQ, D_MODEL), 1.0),
        'cache_attn_k': nrm((DEPTH, DEC_BATCH, PAST_LEN, A_HEADS, 2, A_HEAD_DIM), 1.0),
        'cache_attn_v': nrm((DEPTH, DEC_BATCH, PAST_LEN, A_HEADS, A_VDIM), 1.0),
        'state_gla': nrm((DEPTH, DEC_BATCH, G_HEADS, G_DK, G_DV), 1.0),
        'state_lru_conv': nrm((DEPTH, DEC_BATCH, LRU_CONV - 1, LRU_WIDTH), 1.0),
        'state_lru_h': nrm((DEPTH, DEC_BATCH, LRU_WIDTH), 0.5),
        'state_ffn_conv': nrm((DEPTH, DEC_BATCH, FFN_CONV - 1, D_FF), 1.0),
        'norm_mix': gain((DEPTH, D_MODEL)),
        'w_in': nrm((DEPTH, D_MODEL, PROJ_WIDTH), D_MODEL ** -0.5),
        'lambda_qk': nrm((DEPTH, 4, A_HEAD_DIM), 0.1),
        'attn_subln': gain((DEPTH, A_VDIM)),
        'w_gla_gate2': nrm((DEPTH, G_GATE_RANK, G_HEADS * G_DK), G_GATE_RANK ** -0.5),
        'b_gla_gate': nrm((DEPTH, G_HEADS * G_DK), 0.01),
        'gla_norm': gain((DEPTH, G_DV)),
        'lru_conv_w': nrm((DEPTH, LRU_CONV, LRU_WIDTH), 0.5),
        'lru_conv_b': nrm((DEPTH, LRU_WIDTH), 0.01),
        'lru_wa': nrm((DEPTH, LRU_BLOCKS, LRU_BLOCK, LRU_BLOCK), LRU_BLOCK ** -0.5),
        'lru_ba': nrm((DEPTH, LRU_WIDTH), 0.01),
        'lru_wx': nrm((DEPTH, LRU_BLOCKS, LRU_BLOCK, LRU_BLOCK), LRU_BLOCK ** -0.5),
        'lru_bx': nrm((DEPTH, LRU_WIDTH), 0.01),
        'lru_lambda': lru_lambda,
        'w_branch_attn': nrm((DEPTH, A_WIDTH, D_MODEL), A_WIDTH ** -0.5),
        'w_branch_gla': nrm((DEPTH, G_WIDTH, D_MODEL), G_WIDTH ** -0.5),
        'w_branch_lru': nrm((DEPTH, LRU_WIDTH, D_MODEL), LRU_WIDTH ** -0.5),
        'w_merge': nrm((DEPTH, D_MODEL, N_BRANCH * D_MODEL), D_MODEL ** -0.5),
        'b_merge': nrm((DEPTH, N_BRANCH * D_MODEL), 0.01),
        'w_out': nrm((DEPTH, D_MODEL, D_MODEL), D_MODEL ** -0.5),
        'norm_ffn': gain((DEPTH, D_MODEL)),
        'w_ffn_gate': nrm((DEPTH, D_MODEL, D_FF), D_MODEL ** -0.5),
        'ffn_conv_w': nrm((DEPTH, FFN_CONV, D_FF), FFN_CONV ** -0.5),
        'ffn_conv_b': nrm((DEPTH, D_FF), 0.01),
        'w_ffn_up': nrm((DEPTH, D_MODEL, D_FF), D_MODEL ** -0.5),
        'w_ffn_down': nrm((DEPTH, D_FF, D_MODEL), D_FF ** -0.5),
        'norm_final': gain((D_MODEL,)),
    }


def reference(x_prompt, x_sample, cache_attn_k, cache_attn_v, state_gla, state_lru_conv, state_lru_h, state_ffn_conv,
              norm_mix, w_in, lambda_qk, attn_subln, w_gla_gate2, b_gla_gate, gla_norm, lru_conv_w, lru_conv_b,
              lru_wa, lru_ba, lru_wx, lru_bx, lru_lambda, w_branch_attn, w_branch_gla, w_branch_lru, w_merge, b_merge,
              w_out, norm_ffn, w_ffn_gate, ffn_conv_w, ffn_conv_b, w_ffn_up, w_ffn_down, norm_final):
    layer_params = (norm_mix, w_in, lambda_qk, attn_subln, w_gla_gate2, b_gla_gate, gla_norm, lru_conv_w, lru_conv_b,
                    lru_wa, lru_ba, lru_wx, lru_bx, lru_lambda, w_branch_attn, w_branch_gla, w_branch_lru, w_merge,
                    b_merge, w_out, norm_ffn, w_ffn_gate, ffn_conv_w, ffn_conv_b, w_ffn_up, w_ffn_down)
    bp = x_prompt.shape[0]
    dt = x_prompt.dtype
    zero_k = jnp.zeros((DEPTH, bp, 0, A_HEADS, 2, A_HEAD_DIM), dt)
    zero_v = jnp.zeros((DEPTH, bp, 0, A_HEADS, A_VDIM), dt)
    zero_gla = jnp.zeros((DEPTH, bp, G_HEADS, G_DK, G_DV), jnp.float32)
    zero_lru_conv = jnp.zeros((DEPTH, bp, LRU_CONV - 1, LRU_WIDTH), dt)
    zero_lru_h = jnp.zeros((DEPTH, bp, LRU_WIDTH), jnp.float32)
    zero_ffn_conv = jnp.zeros((DEPTH, bp, FFN_CONV - 1, D_FF), dt)
    y_prompt, (k_p, v_p, gla_p, lconv_p, lh_p, fconv_p) = trunk(
        x_prompt, zero_k, zero_v, zero_gla, zero_lru_conv, zero_lru_h, zero_ffn_conv, layer_params, norm_final)
    y_sample, (k_s, v_s, gla_s, lconv_s, lh_s, fconv_s) = trunk(
        x_sample, cache_attn_k, cache_attn_v, state_gla, state_lru_conv, state_lru_h, state_ffn_conv, layer_params, norm_final)
    return (y_prompt, y_sample, k_p, v_p, gla_p, lconv_p, lh_p, fconv_p, k_s, v_s, gla_s, lconv_s, lh_s, fconv_s)
```

```python
import functools
import math

import jax
import jax.numpy as jnp
from jax import lax
from jax.experimental import pallas as pl
from jax.experimental.pallas import tpu as pltpu

F32 = jnp.float32
BF16 = jnp.bfloat16

D_MODEL = 1024
DEPTH = 4
CHUNK = 64
EPS = 1e-6
A_HEADS = 4
A_HEAD_DIM = 64
A_VDIM = 128
A_WIDTH = A_HEADS * A_VDIM
ROPE_DIM = 16
ROPE_THETA = 500000.0
G_HEADS = 4
G_DK = 64
G_DV = 128
G_QK = G_HEADS * G_DK
G_WIDTH = G_HEADS * G_DV
G_GATE_RANK = 16
G_GATE_TAU = 16.0
LRU_WIDTH = 512
LRU_BLOCKS = 8
LRU_CONV = 4
LRU_C = 8.0
D_FF = 2816
FFN_CONV = 3

LANES = 128
CARRY_ROWS = 8
NEG_BIG = -1e30
VMEM_LIMIT = 56 * 1024 * 1024


def _cparams(sem):
    return pltpu.CompilerParams(dimension_semantics=sem, vmem_limit_bytes=VMEM_LIMIT)


def _rms(x, g):
    return x * lax.rsqrt(jnp.mean(x * x, axis=-1, keepdims=True) + EPS) * g


def _sigmoid(x):
    return 1.0 / (1.0 + jnp.exp(-x))


def _softplus(x):
    return jnp.maximum(x, 0.0) + jnp.log1p(jnp.exp(-jnp.abs(x)))


def _gelu_tanh(x):
    return x * (0.5 * (1.0 + jnp.tanh(math.sqrt(2.0 / math.pi) * (x + 0.044715 * (x * x * x)))))


def _dot(a, b):
    return jnp.dot(a, b, preferred_element_type=F32)


def _dot_nt(a, b):
    return lax.dot_general(a, b, (((1,), (1,)), ((), ())), preferred_element_type=F32)


def _dot_tn(a, b):
    return lax.dot_general(a, b, (((0,), (0,)), ((), ())), preferred_element_type=F32)


def _lane_tile(x, n):
    return x if n == 1 else jnp.concatenate([x] * n, axis=1)


def _rope(x, c, sa, sb):
    w = x.shape[1]
    n = w // LANES
    x_up = pltpu.roll(x, w - ROPE_DIM // 2, 1)
    x_dn = pltpu.roll(x, ROPE_DIM // 2, 1)
    return x * _lane_tile(c, n) + x_up * _lane_tile(sa, n) + x_dn * _lane_tile(sb, n)


def _lin_scan(a, u):
    tm = a.shape[0]
    row = lax.broadcasted_iota(jnp.int32, a.shape, 0)
    d = 1
    while d < tm:
        a_s = pltpu.roll(a, d, 0)
        u_s = pltpu.roll(u, d, 0)
        valid = row >= d
        u = jnp.where(valid, a * u_s + u, u)
        a = jnp.where(valid, a * a_s, a)
        d *= 2
    return a, u


def _proj_kernel(x_ref, cos_ref, sa_ref, sb_ref, nm_ref, wqkv_ref, wgla_ref, wga_ref, wg2_ref, bg2_ref, wlru_ref,
                 cw_ref, cb_ref, wa_ref, ba_ref, wx_ref, bx_ref, lam_ref, buf0_ref, h0_ref,
                 kf_ref, vf_ref, qb_ref, kb_ref, vb_ref, gq_ref, gk_ref, gv_ref, gr_ref, la_ref, lo_ref,
                 bufo_ref, hl_ref, xbuf, hc, *, tm):
    t = pl.program_id(1)
    xn = _rms(x_ref[0], nm_ref[...])
    xb = xn.astype(BF16)

    qkv = _dot(xb, wqkv_ref[...])
    c, sa, sb = cos_ref[...], sa_ref[...], sb_ref[...]
    aw = 2 * A_HEADS * A_HEAD_DIM
    q = _rope(qkv[:, :aw], c, sa, sb) * (A_HEAD_DIM ** -0.5)
    k = _rope(qkv[:, aw:2 * aw], c, sa, sb)
    v = qkv[:, 2 * aw:]
    kf_ref[0] = k
    vf_ref[0] = v
    kb_ref[0] = k.astype(BF16)
    vb_ref[0] = v.astype(BF16)
    lane = lax.broadcasted_iota(jnp.int32, (tm, LANES), 1)
    for h in range(A_HEADS):
        seg = q[:, h * LANES:(h + 1) * LANES]
        qb_ref[0, :, 2 * h * LANES:(2 * h + 1) * LANES] = jnp.where(lane < A_HEAD_DIM, seg, 0.0).astype(BF16)
        qb_ref[0, :, (2 * h + 1) * LANES:(2 * h + 2) * LANES] = jnp.where(lane >= A_HEAD_DIM, seg, 0.0).astype(BF16)

    g = _dot(xb, wgla_ref[...])
    gq_ref[0] = g[:, :G_QK] * (G_DK ** -0.5)
    gk_ref[0] = g[:, G_QK:2 * G_QK]
    gv_ref[0] = g[:, 2 * G_QK:2 * G_QK + G_WIDTH]
    gr_ref[0] = g[:, 2 * G_QK + G_WIDTH:]
    ga = _dot(xb, wga_ref[...])
    gate = _dot(ga.astype(BF16), wg2_ref[...]) + bg2_ref[...]
    la_ref[0] = -_softplus(-gate) * (1.0 / G_GATE_TAU)

    lr = _dot(xb, wlru_ref[...])
    lx = lr[:, :LRU_WIDTH]
    lg = lr[:, LRU_WIDTH:]

    @pl.when(t == 0)
    def _():
        xbuf[0:CARRY_ROWS] = buf0_ref[0]
        hc[...] = h0_ref[0]

    xbuf[CARRY_ROWS:CARRY_ROWS + tm] = lx
    cw = cw_ref[...]
    xc = cb_ref[...]
    for j in range(LRU_CONV):
        o = CARRY_ROWS - (LRU_CONV - 1) + j
        xc = xc + xbuf[o:o + tm] * cw[j:j + 1]
    tail = xbuf[tm:tm + CARRY_ROWS]
    bufo_ref[0] = tail
    xbuf[0:CARRY_ROWS] = tail

    xcb = xc.astype(BF16)
    r = _sigmoid(_dot(xcb, wa_ref[...]) + ba_ref[...])
    ig = _sigmoid(_dot(xcb, wx_ref[...]) + bx_ref[...])
    log_a = (-LRU_C) * r * _softplus(-lam_ref[...])
    a = jnp.exp(log_a)
    u = jnp.sqrt(-jnp.tanh(log_a) * (a * a + 1.0)) * (ig * xc)
    a_cum, h_zero = _lin_scan(a, u)
    hh = a_cum * hc[...] + h_zero
    h_last = hh[tm - 1:tm]
    hc[...] = h_last
    hl_ref[0] = h_last
    lo_ref[0] = (hh * _gelu_tanh(lg)).astype(BF16)


def _proj_call(x, tabs, w, buf0, h0):
    b, t, d = x.shape
    tm = min(t, 256)
    assert t % tm == 0 and tm % 8 == 0
    nt = t // tm
    row = lambda bi, ti: (bi, ti, 0)
    const2 = lambda bi, ti: (0, 0)
    tab_spec = pl.BlockSpec((tm, LANES), lambda bi, ti: (ti, 0))

    def wspec(a):
        return pl.BlockSpec(a.shape, const2)

    weights = (w['norm_mix'], w['wqkv'], w['wgla'], w['wga'], w['wg2'], w['bg2'], w['wlru'], w['lru_conv_w'],
               w['lru_conv_b'], w['wa_bd'], w['lru_ba'], w['wx_bd'], w['lru_bx'], w['lru_lambda'])
    in_specs = ([pl.BlockSpec((1, tm, d), row), tab_spec, tab_spec, tab_spec] + [wspec(a) for a in weights]
                + [pl.BlockSpec((1, CARRY_ROWS, LRU_WIDTH), lambda bi, ti: (bi, 0, 0)),
                   pl.BlockSpec((1, 1, LRU_WIDTH), lambda bi, ti: (bi, 0, 0))])
    aw = 2 * A_HEADS * A_HEAD_DIM

    def out(wd, dt):
        return jax.ShapeDtypeStruct((b, t, wd), dt), pl.BlockSpec((1, tm, wd), row)

    outs = [out(aw, F32), out(A_WIDTH, F32), out(2 * aw, BF16), out(aw, BF16), out(A_WIDTH, BF16),
            out(G_QK, F32), out(G_QK, F32), out(G_WIDTH, F32), out(G_WIDTH, F32), out(G_QK, F32),
            out(LRU_WIDTH, BF16),
            (jax.ShapeDtypeStruct((b, CARRY_ROWS, LRU_WIDTH), F32),
             pl.BlockSpec((1, CARRY_ROWS, LRU_WIDTH), lambda bi, ti: (bi, 0, 0))),
            (jax.ShapeDtypeStruct((b, 1, LRU_WIDTH), F32), pl.BlockSpec((1, 1, LRU_WIDTH), lambda bi, ti: (bi, 0, 0)))]
    return pl.pallas_call(
        functools.partial(_proj_kernel, tm=tm),
        grid=(b, nt),
        in_specs=in_specs,
        out_specs=[o[1] for o in outs],
        out_shape=[o[0] for o in outs],
        scratch_shapes=[pltpu.VMEM((tm + CARRY_ROWS, LRU_WIDTH), F32), pltpu.VMEM((1, LRU_WIDTH), F32)],
        compiler_params=_cparams(("arbitrary", "arbitrary")),
        name="proj",
    )(x, *tabs, *weights, buf0, h0)


def _attn_lambda(lq_ref, lam_init):
    lq = lq_ref[...]
    s1 = jnp.sum(lq[0:1] * lq[1:2], axis=1, keepdims=True)
    s2 = jnp.sum(lq[2:3] * lq[3:4], axis=1, keepdims=True)
    return jnp.exp(s1) - jnp.exp(s2) + lam_init


def _attn_finish(a0, a1, lam, sub, lam_init):
    o = a0[:, :LANES] / a0[:, LANES:] - lam * (a1[:, :LANES] / a1[:, LANES:])
    ms = jnp.mean(o * o, axis=-1, keepdims=True)
    return (o * lax.rsqrt(ms + EPS) * sub * (1.0 - lam_init)).astype(BF16)


def _chunk_mask(q0, k0, tq, tk):
    qpos = q0 + lax.broadcasted_iota(jnp.int32, (tq, tk), 0)
    kpos = k0 + lax.broadcasted_iota(jnp.int32, (tq, tk), 1)
    return (kpos // CHUNK) <= (qpos // CHUNK)


def _attn_prompt_kernel(lq_ref, sub_ref, q_ref, k_ref, v_ref, o_ref, m_sc, acc_sc, *, tq, tk, lam_init):
    i = pl.program_id(2)
    m_sc[...] = jnp.full(m_sc.shape, NEG_BIG, F32)
    acc_sc[...] = jnp.zeros(acc_sc.shape, F32)
    q = q_ref[0]
    qs = (q[:, :LANES], q[:, LANES:])
    ones = jnp.ones((tk, LANES), BF16)

    def step(j, masked):
        r0 = pl.multiple_of(j * tk, tk)
        k = k_ref[0, pl.ds(r0, tk), :]
        vext = jnp.concatenate([v_ref[0, pl.ds(r0, tk), :], ones], axis=1)
        for c in range(2):
            s = _dot_nt(qs[c], k)
            if masked:
                s = jnp.where(_chunk_mask(i * tq, j * tk, tq, tk), s, NEG_BIG)
            m_prev = m_sc[c]
            m_new = jnp.maximum(m_prev, jnp.max(s, axis=1, keepdims=True))
            alpha = jnp.exp(m_prev - m_new)
            p = jnp.exp(s - _lane_tile(m_new, tk // LANES))
            acc_sc[c] = acc_sc[c] * _lane_tile(alpha, 2) + _dot(p.astype(BF16), vext)
            m_sc[c] = m_new

    n_full = (i * tq) // tk

    def body(j, carry):
        step(j, False)
        return carry

    lax.fori_loop(0, n_full, body, 0)
    for jj in range(tq // tk):
        step(n_full + jj, True)
    o_ref[0] = _attn_finish(acc_sc[0], acc_sc[1], _attn_lambda(lq_ref, lam_init), sub_ref[...], lam_init)


def _attn_prompt_call(lq, sub, qb, kb, vb, lam_init):
    b, t, _ = kb.shape
    tq = min(t, 512)
    tk = tq
    assert t % tq == 0 and (tq % CHUNK == 0 or tq == t) and tq % LANES == 0
    return pl.pallas_call(
        functools.partial(_attn_prompt_kernel, tq=tq, tk=tk, lam_init=lam_init),
        grid=(b, A_HEADS, t // tq),
        in_specs=[pl.BlockSpec(lq.shape, lambda bi, h, i: (0, 0)),
                  pl.BlockSpec(sub.shape, lambda bi, h, i: (0, 0)),
                  pl.BlockSpec((1, tq, 2 * LANES), lambda bi, h, i: (bi, i, h)),
                  pl.BlockSpec((1, t, LANES), lambda bi, h, i: (bi, 0, h)),
                  pl.BlockSpec((1, t, LANES), lambda bi, h, i: (bi, 0, h))],
        out_specs=pl.BlockSpec((1, tq, LANES), lambda bi, h, i: (bi, i, h)),
        out_shape=jax.ShapeDtypeStruct((b, t, A_WIDTH), BF16),
        scratch_shapes=[pltpu.VMEM((2, tq, LANES), F32), pltpu.VMEM((2, tq, 2 * LANES), F32)],
        compiler_params=_cparams(("arbitrary", "arbitrary", "arbitrary")),
        name="attn_prompt",
    )(lq, sub, qb, kb, vb)


def _attn_cached_kernel(lq_ref, sub_ref, q_ref, kp_ref, vp_ref, kn_ref, vn_ref, o_ref, *, t, p_len, lam_init):
    q = q_ref[0]
    kp = kp_ref[0].astype(BF16)
    kn = kn_ref[0]
    vp = jnp.concatenate([vp_ref[0].astype(BF16), jnp.ones((p_len, LANES), BF16)], axis=1)
    vn = jnp.concatenate([vn_ref[0], jnp.ones((t, LANES), BF16)], axis=1)
    mask_p = _chunk_mask(p_len, 0, t, p_len)
    mask_n = _chunk_mask(p_len, p_len, t, t)
    accs = []
    for c in range(2):
        qc = q[:, c * LANES:(c + 1) * LANES]
        s_p = jnp.where(mask_p, _dot_nt(qc, kp), NEG_BIG)
        s_n = jnp.where(mask_n, _dot_nt(qc, kn), NEG_BIG)
        m = jnp.maximum(jnp.max(s_p, axis=1, keepdims=True), jnp.max(s_n, axis=1, keepdims=True))
        accs.append(_dot(jnp.exp(s_p - m).astype(BF16), vp) + _dot(jnp.exp(s_n - m).astype(BF16), vn))
    o_ref[0] = _attn_finish(accs[0], accs[1], _attn_lambda(lq_ref, lam_init), sub_ref[...], lam_init)


def _attn_cached_call(lq, sub, qb, kb, vb, k_past, v_past, lam_init):
    b, t, _ = kb.shape
    p_len = k_past.shape[1]
    return pl.pallas_call(
        functools.partial(_attn_cached_kernel, t=t, p_len=p_len, lam_init=lam_init),
        grid=(b, A_HEADS),
        in_specs=[pl.BlockSpec(lq.shape, lambda bi, h: (0, 0)),
                  pl.BlockSpec(sub.shape, lambda bi, h: (0, 0)),
                  pl.BlockSpec((1, t, 2 * LANES), lambda bi, h: (bi, 0, h)),
                  pl.BlockSpec((1, p_len, LANES), lambda bi, h: (bi, 0, h)),
                  pl.BlockSpec((1, p_len, LANES), lambda bi, h: (bi, 0, h)),
                  pl.BlockSpec((1, t, LANES), lambda bi, h: (bi, 0, h)),
                  pl.BlockSpec((1, t, LANES), lambda bi, h: (bi, 0, h))],
        out_specs=pl.BlockSpec((1, t, LANES), lambda bi, h: (bi, 0, h)),
        out_shape=jax.ShapeDtypeStruct((b, t, A_WIDTH), BF16),
        compiler_params=_cparams(("arbitrary", "arbitrary")),
        name="attn_cached",
    )(lq, sub, qb, k_past, v_past, kb, vb)


def _split3(x):
    hi = x.astype(BF16)
    r1 = x - hi.astype(F32)
    mid = r1.astype(BF16)
    lo = (r1 - mid.astype(F32)).astype(BF16)
    return hi, mid, lo


def _gla_kernel(q_ref, k_ref, v_ref, r_ref, la_ref, gn_ref, st0_ref, o_ref, sto_ref, st, *, tg, lc):
    t = pl.program_id(1)

    @pl.when(t == 0)
    def _():
        st[...] = st0_ref[0]

    rr = lax.broadcasted_iota(jnp.int32, (lc, lc), 0)
    cc = lax.broadcasted_iota(jnp.int32, (lc, lc), 1)
    tri = rr >= cc
    tri_b = jnp.where(tri, 1.0, 0.0).astype(BF16)
    lane_k = lax.broadcasted_iota(jnp.int32, (lc, G_QK), 1) // G_DK
    bd = (lax.broadcasted_iota(jnp.int32, (G_WIDTH, G_QK), 0) // G_DV
          == lax.broadcasted_iota(jnp.int32, (G_WIDTH, G_QK), 1) // G_DK)
    gn = gn_ref[...]

    def chunk(ci, carry):
        r0 = pl.multiple_of(ci * lc, lc)
        rows = pl.ds(r0, lc)
        la = la_ref[0, rows, :]
        hi, mid, lo = _split3(la)
        bc = _dot(tri_b, hi) + _dot(tri_b, mid) + _dot(tri_b, lo)
        b_mid = bc[lc // 2 - 1:lc // 2]
        b_last = bc[lc - 1:lc]
        q = q_ref[0, rows, :]
        k = k_ref[0, rows, :]
        vb = v_ref[0, rows, :].astype(BF16)
        q_in = (q * jnp.exp(bc)).astype(BF16)
        q_mid = q * jnp.exp(bc - b_mid)
        k_mid = (k * jnp.exp(b_mid - bc)).astype(BF16)
        k_end = (k * jnp.exp(b_last - bc)).astype(BF16)
        s_cur = st[...]
        inter = _dot_nt(q_in, s_cur.astype(BF16))
        outs = []
        for h in range(G_HEADS):
            qh = jnp.where(lane_k == h, q_mid, 0.0).astype(BF16)
            att = jnp.where(tri, _dot_nt(qh, k_mid), 0.0).astype(BF16)
            intra = _dot(att, vb[:, h * G_DV:(h + 1) * G_DV])
            oh = inter[:, h * G_DV:(h + 1) * G_DV] + intra
            ms = jnp.mean(oh * oh, axis=-1, keepdims=True)
            outs.append(oh * lax.rsqrt(ms + EPS) * gn)
        gr = r_ref[0, rows, :]
        o_ref[0, rows, :] = (jnp.concatenate(outs, axis=1) * (gr * _sigmoid(gr))).astype(BF16)
        st[...] = s_cur * jnp.exp(b_last) + jnp.where(bd, _dot_tn(vb, k_end), 0.0)
        return carry

    lax.fori_loop(0, tg // lc, chunk, 0)
    sto_ref[0] = st[...]


def _gla_call(gq, gk, gv, gr, la, gn, st0):
    b, t, _ = gq.shape
    lc = min(CHUNK, t)
    tg = min(t, 512)
    assert t % tg == 0 and tg % lc == 0 and lc % 16 == 0
    row = lambda bi, ti: (bi, ti, 0)
    st_spec = pl.BlockSpec((1, G_WIDTH, G_QK), lambda bi, ti: (bi, 0, 0))
    return pl.pallas_call(
        functools.partial(_gla_kernel, tg=tg, lc=lc),
        grid=(b, t // tg),
        in_specs=[pl.BlockSpec((1, tg, G_QK), row), pl.BlockSpec((1, tg, G_QK), row),
                  pl.BlockSpec((1, tg, G_WIDTH), row), pl.BlockSpec((1, tg, G_WIDTH), row),
                  pl.BlockSpec((1, tg, G_QK), row), pl.BlockSpec(gn.shape, lambda bi, ti: (0, 0)), st_spec],
        out_specs=[pl.BlockSpec((1, tg, G_WIDTH), row), st_spec],
        out_shape=[jax.ShapeDtypeStruct((b, t, G_WIDTH), BF16), jax.ShapeDtypeStruct((b, G_WIDTH, G_QK), F32)],
        scratch_shapes=[pltpu.VMEM((G_WIDTH, G_QK), F32)],
        compiler_params=_cparams(("arbitrary", "arbitrary")),
        name="gla",
    )(gq, gk, gv, gr, la, gn, st0)


def _merge_kernel(x_ref, ao_ref, go_ref, lo_ref, nm_ref, wm_ref, bm_ref, wba_ref, wbg_ref, wbl_ref, wo_ref, o_ref):
    x = x_ref[...]
    xb = _rms(x, nm_ref[...]).astype(BF16)
    g = _sigmoid(_dot(xb, wm_ref[...]) + bm_ref[...])
    merged = (g[:, :D_MODEL] * _dot(ao_ref[...], wba_ref[...])
              + g[:, D_MODEL:2 * D_MODEL] * _dot(go_ref[...], wbg_ref[...])
              + g[:, 2 * D_MODEL:] * _dot(lo_ref[...], wbl_ref[...]))
    o_ref[...] = x + _dot(merged.astype(BF16), wo_ref[...])


def _merge_call(x2, ao, go, lo, w):
    n, d = x2.shape
    tm = min(n, 256)
    assert n % tm == 0
    row = lambda i: (i, 0)
    weights = (w['norm_mix'], w['w_merge'], w['b_merge'], w['w_branch_attn'], w['w_branch_gla'], w['w_branch_lru'],
               w['w_out'])
    return pl.pallas_call(
        _merge_kernel,
        grid=(n // tm,),
        in_specs=[pl.BlockSpec((tm, d), row), pl.BlockSpec((tm, A_WIDTH), row), pl.BlockSpec((tm, G_WIDTH), row),
                  pl.BlockSpec((tm, LRU_WIDTH), row)] + [pl.BlockSpec(a.shape, lambda i: (0, 0)) for a in weights],
        out_specs=pl.BlockSpec((tm, d), row),
        out_shape=jax.ShapeDtypeStruct((n, d), F32),
        compiler_params=_cparams(("arbitrary",)),
        name="merge",
    )(x2, ao, go, lo, *weights)


def _ffn_kernel(x_ref, nf_ref, wg_ref, cw_ref, cb_ref, wu_ref, wd_ref, nl_ref, buf0_ref, o_ref, bufo_ref, ubuf,
                *, tm, final):
    t = pl.program_id(1)
    x = x_ref[0]
    hb = _rms(x, nf_ref[...]).astype(BF16)

    @pl.when(t == 0)
    def _():
        ubuf[0:CARRY_ROWS] = buf0_ref[0]

    ubuf[CARRY_ROWS:CARRY_ROWS + tm] = _dot(hb, wg_ref[...])
    cw = cw_ref[...]
    gc = cb_ref[...]
    for j in range(FFN_CONV):
        o = CARRY_ROWS - (FFN_CONV - 1) + j
        gc = gc + ubuf[o:o + tm] * cw[j:j + 1]
    tail = ubuf[tm:tm + CARRY_ROWS]
    bufo_ref[0] = tail
    ubuf[0:CARRY_ROWS] = tail
    f = _gelu_tanh(gc) * _dot(hb, wu_ref[...])
    y = x + _dot(f.astype(BF16), wd_ref[...])
    o_ref[0] = _rms(y, nl_ref[...]) if final else y


def _ffn_call(x, w, norm_last, buf0, final):
    b, t, d = x.shape
    tm = min(t, 256)
    assert t % tm == 0
    row = lambda bi, ti: (bi, ti, 0)
    const2 = lambda bi, ti: (0, 0)
    buf_spec = pl.BlockSpec((1, CARRY_ROWS, D_FF), lambda bi, ti: (bi, 0, 0))
    weights = (w['norm_ffn'], w['w_ffn_gate'], w['ffn_conv_w'], w['ffn_conv_b'], w['w_ffn_up'], w['w_ffn_down'],
               norm_last)
    return pl.pallas_call(
        functools.partial(_ffn_kernel, tm=tm, final=final),
        grid=(b, t // tm),
        in_specs=[pl.BlockSpec((1, tm, d), row)] + [pl.BlockSpec(a.shape, const2) for a in weights] + [buf_spec],
        out_specs=[pl.BlockSpec((1, tm, d), row), buf_spec],
        out_shape=[jax.ShapeDtypeStruct((b, t, d), F32), jax.ShapeDtypeStruct((b, CARRY_ROWS, D_FF), F32)],
        scratch_shapes=[pltpu.VMEM((tm + CARRY_ROWS, D_FF), F32)],
        compiler_params=_cparams(("arbitrary", "arbitrary")),
        name="ffn",
    )(x, *weights, buf0)


def _rope_tables(p_len, t):
    half = ROPE_DIM // 2
    pos = (p_len + jnp.arange(t, dtype=jnp.int32)).astype(F32)
    inv = ROPE_THETA ** (-jnp.arange(half, dtype=F32) / half)
    ang = pos[:, None] * inv[None, :]
    cos, sin = jnp.cos(ang), jnp.sin(ang)
    rest = A_HEAD_DIM - ROPE_DIM
    z = jnp.zeros((t, half), F32)
    c64 = jnp.concatenate([cos, cos, jnp.ones((t, rest), F32)], axis=1)
    sa64 = jnp.concatenate([-sin, z, jnp.zeros((t, rest), F32)], axis=1)
    sb64 = jnp.concatenate([z, sin, jnp.zeros((t, rest), F32)], axis=1)
    rep = LANES // A_HEAD_DIM
    return tuple(jnp.tile(a, (1, rep)) for a in (c64, sa64, sb64))


def _block_diag(wb):
    n, bi, bo = wb.shape
    eye = jnp.eye(n, dtype=wb.dtype)
    return (eye[:, None, :, None] * wb[:, :, None, :]).reshape(n * bi, n * bo)


def _prep_layer(l, p):
    aw = 2 * A_HEADS * A_HEAD_DIM
    o_gla = 2 * aw + A_WIDTH
    o_ga = o_gla + 2 * G_QK + 2 * G_WIDTH
    o_lru = o_ga + G_GATE_RANK
    w_in = p['w_in'][l]
    row = lambda a: a[l].reshape(1, -1)
    return {
        'norm_mix': row(p['norm_mix']),
        'wqkv': w_in[:, :o_gla].astype(BF16),
        'wgla': w_in[:, o_gla:o_ga].astype(BF16),
        'wga': w_in[:, o_ga:o_lru].astype(BF16),
        'wlru': w_in[:, o_lru:].astype(BF16),
        'wg2': p['w_gla_gate2'][l].astype(BF16),
        'bg2': row(p['b_gla_gate']),
        'lambda_qk': p['lambda_qk'][l],
        'attn_subln': row(p['attn_subln']),
        'gla_norm': row(p['gla_norm']),
        'lru_conv_w': p['lru_conv_w'][l],
        'lru_conv_b': row(p['lru_conv_b']),
        'wa_bd': _block_diag(p['lru_wa'][l]).astype(BF16),
        'lru_ba': row(p['lru_ba']),
        'wx_bd': _block_diag(p['lru_wx'][l]).astype(BF16),
        'lru_bx': row(p['lru_bx']),
        'lru_lambda': row(p['lru_lambda']),
        'w_branch_attn': p['w_branch_attn'][l].astype(BF16),
        'w_branch_gla': p['w_branch_gla'][l].astype(BF16),
        'w_branch_lru': p['w_branch_lru'][l].astype(BF16),
        'w_merge': p['w_merge'][l].astype(BF16),
        'b_merge': row(p['b_merge']),
        'w_out': p['w_out'][l].astype(BF16),
        'norm_ffn': row(p['norm_ffn']),
        'w_ffn_gate': p['w_ffn_gate'][l].astype(BF16),
        'ffn_conv_w': p['ffn_conv_w'][l],
        'ffn_conv_b': row(p['ffn_conv_b']),
        'w_ffn_up': p['w_ffn_up'][l].astype(BF16),
        'w_ffn_down': p['w_ffn_down'][l].astype(BF16),
    }


def _pad_carry(buf):
    return jnp.pad(buf, ((0, 0), (CARRY_ROWS - buf.shape[1], 0), (0, 0)))


def _state_to_kernel(s):
    b = s.shape[0]
    eye = jnp.eye(G_HEADS, dtype=s.dtype)
    st = jnp.swapaxes(s, 2, 3)
    return (st[:, :, :, None, :] * eye[None, :, None, :, None]).reshape(b, G_WIDTH, G_QK)


def _state_from_kernel(st):
    b = st.shape[0]
    s5 = st.reshape(b, G_HEADS, G_DV, G_HEADS, G_DK)
    diag = jnp.stack([s5[:, h, :, h, :] for h in range(G_HEADS)], axis=1)
    return jnp.swapaxes(diag, 2, 3)


def _trunk(x, caches, layers, norm_final):
    cache_k, cache_v, st_gla, st_lconv, st_lh, st_fconv = caches
    b, t, d = x.shape
    p_len = 0 if cache_k is None else cache_k.shape[2]
    tabs = _rope_tables(p_len, t)
    outs = [[], [], [], [], [], []]
    for l, w in enumerate(layers):
        lam_init = 0.8 - 0.6 * math.exp(-0.3 * l)
        (kf, vf, qb, kb, vb, gq, gk, gv, gr, la, lo, lbuf, hl) = _proj_call(
            x, tabs, w, _pad_carry(st_lconv[l]), st_lh[l].reshape(b, 1, LRU_WIDTH))
        if cache_k is None:
            ao = _attn_prompt_call(w['lambda_qk'], w['attn_subln'], qb, kb, vb, lam_init)
        else:
            ao = _attn_cached_call(w['lambda_qk'], w['attn_subln'], qb, kb, vb,
                                   cache_k[l].reshape(b, p_len, -1), cache_v[l].reshape(b, p_len, -1), lam_init)
        go, st_new = _gla_call(gq, gk, gv, gr, la, w['gla_norm'], _state_to_kernel(st_gla[l]))
        x1 = _merge_call(x.reshape(b * t, d), ao.reshape(b * t, -1), go.reshape(b * t, -1), lo.reshape(b * t, -1), w)
        x, fbuf = _ffn_call(x1.reshape(b, t, d), w, norm_final.reshape(1, -1), _pad_carry(st_fconv[l]),
                            final=(l == len(layers) - 1))
        outs[0].append(kf.reshape(b, t, A_HEADS, 2, A_HEAD_DIM))
        outs[1].append(vf.reshape(b, t, A_HEADS, A_VDIM))
        outs[2].append(_state_from_kernel(st_new))
        outs[3].append(lbuf[:, CARRY_ROWS - (LRU_CONV - 1):])
        outs[4].append(hl.reshape(b, LRU_WIDTH))
        outs[5].append(fbuf[:, CARRY_ROWS - (FFN_CONV - 1):])
    return x, [jnp.stack(o) for o in outs]


@jax.jit
def _forward(x_prompt, x_sample, cache_attn_k, cache_attn_v, state_gla, state_lru_conv, state_lru_h, state_ffn_conv,
             params, norm_final):
    layers = [_prep_layer(l, params) for l in range(DEPTH)]
    bp = x_prompt.shape[0]
    zeros = (None, None,
             jnp.zeros((DEPTH, bp, G_HEADS, G_DK, G_DV), F32),
             jnp.zeros((DEPTH, bp, LRU_CONV - 1, LRU_WIDTH), F32),
             jnp.zeros((DEPTH, bp, LRU_WIDTH), F32),
             jnp.zeros((DEPTH, bp, FFN_CONV - 1, D_FF), F32))
    y_p, new_p = _trunk(x_prompt, zeros, layers, norm_final)
    y_s, new_s = _trunk(x_sample, (cache_attn_k, cache_attn_v, state_gla, state_lru_conv, state_lru_h,
                                   state_ffn_conv), layers, norm_final)
    return (y_p, y_s, *new_p, *new_s)


def kernel(x_prompt, x_sample, cache_attn_k, cache_attn_v, state_gla, state_lru_conv, state_lru_h, state_ffn_conv, norm_mix, w_in, lambda_qk, attn_subln, w_gla_gate2, b_gla_gate, gla_norm, lru_conv_w, lru_conv_b, lru_wa, lru_ba, lru_wx, lru_bx, lru_lambda, w_branch_attn, w_branch_gla, w_branch_lru, w_merge, b_merge, w_out, norm_ffn, w_ffn_gate, ffn_conv_w, ffn_conv_b, w_ffn_up, w_ffn_down, norm_final):
    params = dict(norm_mix=norm_mix, w_in=w_in, lambda_qk=lambda_qk, attn_subln=attn_subln, w_gla_gate2=w_gla_gate2,
                  b_gla_gate=b_gla_gate, gla_norm=gla_norm, lru_conv_w=lru_conv_w, lru_conv_b=lru_conv_b,
                  lru_wa=lru_wa, lru_ba=lru_ba, lru_wx=lru_wx, lru_bx=lru_bx, lru_lambda=lru_lambda,
                  w_branch_attn=w_branch_attn, w_branch_gla=w_branch_gla, w_branch_lru=w_branch_lru,
                  w_merge=w_merge, b_merge=b_merge, w_out=w_out, norm_ffn=norm_ffn, w_ffn_gate=w_ffn_gate,
                  ffn_conv_w=ffn_conv_w, ffn_conv_b=ffn_conv_b, w_ffn_up=w_ffn_up, w_ffn_down=w_ffn_down)
    return _forward(x_prompt, x_sample, cache_attn_k, cache_attn_v, state_gla, state_lru_conv, state_lru_h,
                    state_ffn_conv, params, norm_final)
```

```python
import functools
import math

import jax
import jax.numpy as jnp
from jax import lax
from jax.experimental import pallas as pl
from jax.experimental.pallas import tpu as pltpu

F32 = jnp.float32
BF16 = jnp.bfloat16

D_MODEL = 1024
DEPTH = 4
CHUNK = 64
EPS = 1e-6
A_HEADS = 4
A_HEAD_DIM = 64
A_VDIM = 128
A_WIDTH = A_HEADS * A_VDIM
ROPE_DIM = 16
ROPE_THETA = 500000.0
G_HEADS = 4
G_DK = 64
G_DV = 128
G_QK = G_HEADS * G_DK
G_WIDTH = G_HEADS * G_DV
G_GATE_RANK = 16
G_GATE_TAU = 16.0
LRU_WIDTH = 512
LRU_BLOCKS = 8
LRU_CONV = 4
LRU_C = 8.0
D_FF = 2816
FFN_CONV = 3

LANES = 128
SUBLANES = 8
BF16_SUBLANES = 16
LOG2E = math.log2(math.e)
CARRY_ROWS = 8
NEG_BIG = -1e30
VMEM_LIMIT = 56 * 1024 * 1024


def _cparams(sem):
    return pltpu.CompilerParams(dimension_semantics=sem, vmem_limit_bytes=VMEM_LIMIT)


def _rms(x, g):
    return x * lax.rsqrt(jnp.mean(x * x, axis=-1, keepdims=True) + EPS) * g


def _sigmoid(x):
    return 1.0 / (1.0 + jnp.exp(-x))


def _softplus(x):
    return jnp.maximum(x, 0.0) + jnp.log1p(jnp.exp(-jnp.abs(x)))


def _gelu_tanh(x):
    return x * (0.5 * (1.0 + jnp.tanh(math.sqrt(2.0 / math.pi) * (x + 0.044715 * (x * x * x)))))


def _dot(a, b):
    return jnp.dot(a, b, preferred_element_type=F32)


def _dot_nt(a, b):
    return lax.dot_general(a, b, (((1,), (1,)), ((), ())), preferred_element_type=F32)


def _dot_tn(a, b):
    return lax.dot_general(a, b, (((0,), (0,)), ((), ())), preferred_element_type=F32)


def _lane_tile(x, n):
    return x if n == 1 else jnp.concatenate([x] * n, axis=1)


def _rope(x, c, sa, sb):
    segs = []
    for g in range(x.shape[1] // LANES):
        seg = x[:, g * LANES:(g + 1) * LANES]
        seg_up = pltpu.roll(seg, LANES - ROPE_DIM // 2, 1)
        seg_dn = pltpu.roll(seg, ROPE_DIM // 2, 1)
        segs.append(seg * c + seg_up * sa + seg_dn * sb)
    return jnp.concatenate(segs, axis=1)


def _lin_scan(a, u, h_in):
    tm, width = a.shape
    groups = tm // SUBLANES
    a = a.reshape(groups, SUBLANES, width)
    u = u.reshape(groups, SUBLANES, width)
    sub = lax.broadcasted_iota(jnp.int32, a.shape, 1)
    d = 1
    while d < SUBLANES:
        a_s = pltpu.roll(a, d, 1)
        u_s = pltpu.roll(u, d, 1)
        valid = sub >= d
        u = jnp.where(valid, a * u_s + u, u)
        a = jnp.where(valid, a * a_s, a)
        d *= 2
    carry = h_in
    out = []
    for g in range(groups):
        hg = a[g] * carry + u[g]
        out.append(hg)
        carry = hg[SUBLANES - 1:SUBLANES]
    return jnp.concatenate(out, axis=0)


def _proj_kernel(x_ref, cos_ref, sa_ref, sb_ref, nm_ref, wqkv_ref, wgla_ref, wga_ref, wg2_ref, bg2_ref, wlru_ref,
                 cw_ref, cb_ref, wa_ref, ba_ref, wx_ref, bx_ref, lam_ref, buf0_ref, h0_ref,
                 kf_ref, vf_ref, qb_ref, kb_ref, vt_ref, gq_ref, gk_ref, gv_ref, gr_ref, la_ref, lo_ref,
                 bufo_ref, hl_ref, xbuf, hc, *, tm):
    t = pl.program_id(1)
    xn = _rms(x_ref[0], nm_ref[...])
    xb = xn.astype(BF16)

    lr = _dot(xb, wlru_ref[...])
    lx = lr[:, :LRU_WIDTH]
    lg = lr[:, LRU_WIDTH:]

    @pl.when(t == 0)
    def _():
        xbuf[0:CARRY_ROWS] = buf0_ref[0]
        hc[...] = h0_ref[0]

    xbuf[CARRY_ROWS:CARRY_ROWS + tm] = lx
    cw = cw_ref[...]
    xc = cb_ref[...]
    for j in range(LRU_CONV):
        o = CARRY_ROWS - (LRU_CONV - 1) + j
        xc = xc + xbuf[o:o + tm] * cw[j:j + 1]
    tail = xbuf[tm:tm + CARRY_ROWS]
    bufo_ref[0] = tail
    xbuf[0:CARRY_ROWS] = tail

    xcb = xc.astype(BF16)
    r = _sigmoid(_dot(xcb, wa_ref[...]) + ba_ref[...])
    ig = _sigmoid(_dot(xcb, wx_ref[...]) + bx_ref[...])
    log_a = (-LRU_C) * r * _softplus(-lam_ref[...])
    a = jnp.exp(log_a)
    u = jnp.sqrt(-jnp.tanh(log_a) * (a * a + 1.0)) * (ig * xc)
    hh = _lin_scan(a, u, hc[...])
    h_last = hh[tm - 1:tm]
    hc[...] = h_last
    hl_ref[0] = h_last
    lo_ref[0] = (hh * _gelu_tanh(lg)).astype(BF16)

    qkv = _dot(xb, wqkv_ref[...])
    c, sa, sb = cos_ref[...], sa_ref[...], sb_ref[...]
    aw = 2 * A_HEADS * A_HEAD_DIM
    q = _rope(qkv[:, :aw], c, sa, sb) * (A_HEAD_DIM ** -0.5 * LOG2E)
    k = _rope(qkv[:, aw:2 * aw], c, sa, sb)
    v = qkv[:, 2 * aw:]
    kf_ref[0] = k
    vf_ref[0] = v
    kb_ref[0] = k.astype(BF16)
    vt_ref[0, :, 0] = v.T.reshape(A_HEADS, A_VDIM, tm).astype(BF16)
    lane = lax.broadcasted_iota(jnp.int32, (tm, LANES), 1)
    for h in range(A_HEADS):
        seg = q[:, h * LANES:(h + 1) * LANES]
        qb_ref[0, :, 2 * h * LANES:(2 * h + 1) * LANES] = jnp.where(lane < A_HEAD_DIM, seg, 0.0).astype(BF16)
        qb_ref[0, :, (2 * h + 1) * LANES:(2 * h + 2) * LANES] = jnp.where(lane >= A_HEAD_DIM, seg, 0.0).astype(BF16)

    g = _dot(xb, wgla_ref[...])
    gq_ref[0] = g[:, :G_QK] * (G_DK ** -0.5)
    gk_ref[0] = g[:, G_QK:2 * G_QK]
    gv_ref[0] = g[:, 2 * G_QK:2 * G_QK + G_WIDTH].astype(BF16)
    gr_ref[0] = g[:, 2 * G_QK + G_WIDTH:]
    ga = _dot(xb, wga_ref[...])
    gate = _dot(ga.astype(BF16), wg2_ref[...]) + bg2_ref[...]
    la_ref[0] = -_softplus(-gate) * (1.0 / G_GATE_TAU)


def _proj_call(x, tabs, w, buf0, h0):
    b, t, d = x.shape
    tm = min(t, 256)
    assert t % tm == 0 and tm % 8 == 0
    nt = t // tm
    row = lambda bi, ti: (bi, ti, 0)
    const2 = lambda bi, ti: (0, 0)
    tab_spec = pl.BlockSpec((tm, LANES), lambda bi, ti: (ti, 0))

    def wspec(a):
        return pl.BlockSpec(a.shape, const2)

    weights = (w['norm_mix'], w['wqkv'], w['wgla'], w['wga'], w['wg2'], w['bg2'], w['wlru'], w['lru_conv_w'],
               w['lru_conv_b'], w['wa_bd'], w['lru_ba'], w['wx_bd'], w['lru_bx'], w['lru_lambda'])
    in_specs = ([pl.BlockSpec((1, tm, d), row), tab_spec, tab_spec, tab_spec] + [wspec(a) for a in weights]
                + [pl.BlockSpec((1, CARRY_ROWS, LRU_WIDTH), lambda bi, ti: (bi, 0, 0)),
                   pl.BlockSpec((1, 1, LRU_WIDTH), lambda bi, ti: (bi, 0, 0))])
    aw = 2 * A_HEADS * A_HEAD_DIM

    def out(wd, dt):
        return jax.ShapeDtypeStruct((b, t, wd), dt), pl.BlockSpec((1, tm, wd), row)

    vt_out = (jax.ShapeDtypeStruct((b, A_HEADS, nt, A_VDIM, tm), BF16),
              pl.BlockSpec((1, A_HEADS, 1, A_VDIM, tm), lambda bi, ti: (bi, 0, ti, 0, 0)))
    outs = [out(aw, F32), out(A_WIDTH, F32), out(2 * aw, BF16), out(aw, BF16), vt_out,
            out(G_QK, F32), out(G_QK, F32), out(G_WIDTH, BF16), out(G_WIDTH, F32), out(G_QK, F32),
            out(LRU_WIDTH, BF16),
            (jax.ShapeDtypeStruct((b, CARRY_ROWS, LRU_WIDTH), F32),
             pl.BlockSpec((1, CARRY_ROWS, LRU_WIDTH), lambda bi, ti: (bi, 0, 0))),
            (jax.ShapeDtypeStruct((b, 1, LRU_WIDTH), F32), pl.BlockSpec((1, 1, LRU_WIDTH), lambda bi, ti: (bi, 0, 0)))]
    return pl.pallas_call(
        functools.partial(_proj_kernel, tm=tm),
        grid=(b, nt),
        in_specs=in_specs,
        out_specs=[o[1] for o in outs],
        out_shape=[o[0] for o in outs],
        scratch_shapes=[pltpu.VMEM((tm + CARRY_ROWS, LRU_WIDTH), F32), pltpu.VMEM((1, LRU_WIDTH), F32)],
        compiler_params=_cparams(("arbitrary", "arbitrary")),
        name="proj",
    )(x, *tabs, *weights, buf0, h0)


def _attn_lambda(lq_ref, lam_init):
    lq = lq_ref[...]
    s1 = jnp.sum(lq[0:1] * lq[1:2], axis=1, keepdims=True)
    s2 = jnp.sum(lq[2:3] * lq[3:4], axis=1, keepdims=True)
    return jnp.exp(s1) - jnp.exp(s2) + lam_init


def _attn_norm(o, sub, lam_init):
    ms = jnp.mean(o * o, axis=-1, keepdims=True)
    return (o * lax.rsqrt(ms + EPS) * sub * (1.0 - lam_init)).astype(BF16)


def _chunk_mask(q0, k0, tq, tk, keys_on_rows=False):
    shape = (tk, tq) if keys_on_rows else (tq, tk)
    qpos = q0 + lax.broadcasted_iota(jnp.int32, shape, 1 if keys_on_rows else 0)
    kpos = k0 + lax.broadcasted_iota(jnp.int32, shape, 0 if keys_on_rows else 1)
    return (kpos // CHUNK) <= (qpos // CHUNK)


def _attn_prompt_kernel(lq_ref, sub_ref, q_ref, k_ref, vt_ref, o_ref, m_sc, acc_sc, s_sc, mt_sc,
                        *, tq, tk, tv, lam_init):
    i = pl.program_id(2)
    m_sc[...] = jnp.full(m_sc.shape, NEG_BIG, F32)
    acc_sc[...] = jnp.zeros(acc_sc.shape, F32)
    q = q_ref[0]
    qs = (q[:, :LANES], q[:, LANES:])
    ones = jnp.ones((BF16_SUBLANES, tk), BF16)

    def scores(t, slot, masked):
        k = k_ref[0, pl.ds(pl.multiple_of(t * tk, tk), tk), :]
        for c in range(2):
            s = _dot_nt(k, qs[c])
            if masked:
                s = jnp.where(_chunk_mask(i * tq, t * tk, tq, tk, keys_on_rows=True), s, NEG_BIG)
            s_sc[slot, c] = s
            mt_sc[slot, c] = jnp.max(s, axis=0, keepdims=True)

    def consume(t, slot):
        vt = [vt_ref[0, 0, t * (tk // tv) + u] for u in range(tk // tv)]
        vt_aug = jnp.concatenate([_lane_cat(vt), ones], axis=0)
        for c in range(2):
            m_prev = m_sc[c]
            m_new = jnp.maximum(m_prev, mt_sc[slot, c])
            alpha = jnp.exp2(m_prev - m_new)
            p = jnp.exp2(s_sc[slot, c] - m_new).astype(BF16)
            acc_sc[c] = acc_sc[c] * alpha + _dot(vt_aug, p)
            m_sc[c] = m_new

    n_full = i
    n_plain = jnp.maximum(n_full - 1, 0)
    scores(0, 0, True)

    def pair(p, carry):
        t = 2 * p
        scores(t + 1, 1, False)
        consume(t, 0)
        scores(t + 2, 0, False)
        consume(t + 1, 1)
        return carry

    lax.fori_loop(0, n_plain // 2, pair, 0)
    tc = (n_plain // 2) * 2

    @pl.when(n_full == 0)
    def _():
        consume(0, 0)

    @pl.when(jnp.logical_and(n_full > 0, n_plain % 2 == 0))
    def _():
        scores(tc + 1, 1, True)
        consume(tc, 0)
        consume(tc + 1, 1)

    @pl.when(jnp.logical_and(n_full > 0, n_plain % 2 == 1))
    def _():
        scores(tc + 1, 1, False)
        consume(tc, 0)
        scores(tc + 2, 0, True)
        consume(tc + 1, 1)
        consume(tc + 2, 0)

    a0, a1 = acc_sc[0], acc_sc[1]
    lam = _attn_lambda(lq_ref, lam_init)
    o_t = a0[:A_VDIM] / a0[A_VDIM:A_VDIM + 1] - lam * (a1[:A_VDIM] / a1[A_VDIM:A_VDIM + 1])
    o_ref[0] = _attn_norm(o_t.T, sub_ref[...], lam_init)


def _lane_cat(parts):
    return parts[0] if len(parts) == 1 else jnp.concatenate(parts, axis=1)


def _attn_prompt_call(lq, sub, qb, kb, vt, lam_init):
    b, t, _ = kb.shape
    tv = vt.shape[-1]
    tq = min(t, 512)
    tk = tq
    assert t % tq == 0 and (tq % CHUNK == 0 or tq == t) and tq % LANES == 0 and tk % tv == 0
    return pl.pallas_call(
        functools.partial(_attn_prompt_kernel, tq=tq, tk=tk, tv=tv, lam_init=lam_init),
        grid=(b, A_HEADS, t // tq),
        in_specs=[pl.BlockSpec(lq.shape, lambda bi, h, i: (0, 0)),
                  pl.BlockSpec(sub.shape, lambda bi, h, i: (0, 0)),
                  pl.BlockSpec((1, tq, 2 * LANES), lambda bi, h, i: (bi, i, h)),
                  pl.BlockSpec((1, t, LANES), lambda bi, h, i: (bi, 0, h)),
                  pl.BlockSpec((1, 1, t // tv, A_VDIM, tv), lambda bi, h, i: (bi, h, 0, 0, 0))],
        out_specs=pl.BlockSpec((1, tq, LANES), lambda bi, h, i: (bi, i, h)),
        out_shape=jax.ShapeDtypeStruct((b, t, A_WIDTH), BF16),
        scratch_shapes=[pltpu.VMEM((2, 1, tq), F32), pltpu.VMEM((2, A_VDIM + BF16_SUBLANES, tq), F32),
                        pltpu.VMEM((2, 2, tk, tq), F32), pltpu.VMEM((2, 2, 1, tq), F32)],
        compiler_params=_cparams(("arbitrary", "arbitrary", "arbitrary")),
        name="attn_prompt",
    )(lq, sub, qb, kb, vt)


def _attn_cached_kernel(lq_ref, sub_ref, q_ref, kp_ref, vp_ref, kn_ref, vn_ref, o_ref, *, t, p_len, lam_init):
    q = q_ref[0]
    kp = kp_ref[0].astype(BF16)
    kn = kn_ref[0].astype(BF16)
    vp = jnp.concatenate([vp_ref[0].astype(BF16), jnp.ones((p_len, LANES), BF16)], axis=1)
    vn = jnp.concatenate([vn_ref[0].astype(BF16), jnp.ones((t, LANES), BF16)], axis=1)
    mask_p = _chunk_mask(p_len, 0, t, p_len)
    mask_n = _chunk_mask(p_len, p_len, t, t)
    accs = []
    for c in range(2):
        qc = q[:, c * LANES:(c + 1) * LANES]
        s_p = jnp.where(mask_p, _dot_nt(qc, kp), NEG_BIG)
        s_n = jnp.where(mask_n, _dot_nt(qc, kn), NEG_BIG)
        m = jnp.maximum(jnp.max(s_p, axis=1, keepdims=True), jnp.max(s_n, axis=1, keepdims=True))
        accs.append(_dot(jnp.exp2(s_p - m).astype(BF16), vp) + _dot(jnp.exp2(s_n - m).astype(BF16), vn))
    a0, a1 = accs
    o = a0[:, :LANES] / a0[:, LANES:] - _attn_lambda(lq_ref, lam_init) * (a1[:, :LANES] / a1[:, LANES:])
    o_ref[0] = _attn_norm(o, sub_ref[...], lam_init)


def _attn_cached_call(lq, sub, qb, k_new, v_new, k_past, v_past, lam_init):
    b, t, _ = k_new.shape
    p_len = k_past.shape[1]
    return pl.pallas_call(
        functools.partial(_attn_cached_kernel, t=t, p_len=p_len, lam_init=lam_init),
        grid=(b, A_HEADS),
        in_specs=[pl.BlockSpec(lq.shape, lambda bi, h: (0, 0)),
                  pl.BlockSpec(sub.shape, lambda bi, h: (0, 0)),
                  pl.BlockSpec((1, t, 2 * LANES), lambda bi, h: (bi, 0, h)),
                  pl.BlockSpec((1, p_len, LANES), lambda bi, h: (bi, 0, h)),
                  pl.BlockSpec((1, p_len, LANES), lambda bi, h: (bi, 0, h)),
                  pl.BlockSpec((1, t, LANES), lambda bi, h: (bi, 0, h)),
                  pl.BlockSpec((1, t, LANES), lambda bi, h: (bi, 0, h))],
        out_specs=pl.BlockSpec((1, t, LANES), lambda bi, h: (bi, 0, h)),
        out_shape=jax.ShapeDtypeStruct((b, t, A_WIDTH), BF16),
        compiler_params=_cparams(("arbitrary", "arbitrary")),
        name="attn_cached",
    )(lq, sub, qb, k_past, v_past, k_new, v_new)


def _split3(x):
    hi = x.astype(BF16)
    r1 = x - hi.astype(F32)
    mid = r1.astype(BF16)
    lo = (r1 - mid.astype(F32)).astype(BF16)
    return hi, mid, lo


def _gla_kernel(q_ref, k_ref, v_ref, r_ref, la_ref, gn_ref, st0_ref, o_ref, sto_ref, st, *, tg, lc):
    t = pl.program_id(1)

    @pl.when(t == 0)
    def _():
        st[...] = st0_ref[0]

    rr = lax.broadcasted_iota(jnp.int32, (lc, lc), 0)
    cc = lax.broadcasted_iota(jnp.int32, (lc, lc), 1)
    tri = rr >= cc
    tri_b = jnp.where(tri, 1.0, 0.0).astype(BF16)
    lane_k = lax.broadcasted_iota(jnp.int32, (lc, G_QK), 1) // G_DK
    bd = (lax.broadcasted_iota(jnp.int32, (G_WIDTH, G_QK), 0) // G_DV
          == lax.broadcasted_iota(jnp.int32, (G_WIDTH, G_QK), 1) // G_DK)
    gn = gn_ref[...]

    def chunk(ci, carry):
        r0 = pl.multiple_of(ci * lc, lc)
        rows = pl.ds(r0, lc)
        la = la_ref[0, rows, :]
        hi, mid, lo = _split3(la)
        bc = _dot(tri_b, hi) + _dot(tri_b, mid) + _dot(tri_b, lo)
        b_mid = bc[lc // 2 - 1:lc // 2]
        b_last = bc[lc - 1:lc]
        q = q_ref[0, rows, :]
        k = k_ref[0, rows, :]
        vb = v_ref[0, rows, :]
        q_in = (q * jnp.exp(bc)).astype(BF16)
        q_mid = q * jnp.exp(bc - b_mid)
        k_mid = (k * jnp.exp(b_mid - bc)).astype(BF16)
        k_end = (k * jnp.exp(b_last - bc)).astype(BF16)
        s_cur = st[...]
        inter = _dot_nt(q_in, s_cur.astype(BF16))
        outs = []
        for h in range(G_HEADS):
            qh = jnp.where(lane_k == h, q_mid, 0.0).astype(BF16)
            att = jnp.where(tri, _dot_nt(qh, k_mid), 0.0).astype(BF16)
            intra = _dot(att, vb[:, h * G_DV:(h + 1) * G_DV])
            oh = inter[:, h * G_DV:(h + 1) * G_DV] + intra
            ms = jnp.mean(oh * oh, axis=-1, keepdims=True)
            outs.append(oh * lax.rsqrt(ms + EPS) * gn)
        gr = r_ref[0, rows, :]
        o_ref[0, rows, :] = (jnp.concatenate(outs, axis=1) * (gr * _sigmoid(gr))).astype(BF16)
        st[...] = s_cur * jnp.exp(b_last) + jnp.where(bd, _dot_tn(vb, k_end), 0.0)
        return carry

    lax.fori_loop(0, tg // lc, chunk, 0, unroll=True)
    sto_ref[0] = st[...]


def _gla_call(gq, gk, gv, gr, la, gn, st0):
    b, t, _ = gq.shape
    lc = min(CHUNK, t)
    tg = min(t, 512)
    assert t % tg == 0 and tg % lc == 0 and lc % 16 == 0
    row = lambda bi, ti: (bi, ti, 0)
    st_spec = pl.BlockSpec((1, G_WIDTH, G_QK), lambda bi, ti: (bi, 0, 0))
    return pl.pallas_call(
        functools.partial(_gla_kernel, tg=tg, lc=lc),
        grid=(b, t // tg),
        in_specs=[pl.BlockSpec((1, tg, G_QK), row), pl.BlockSpec((1, tg, G_QK), row),
                  pl.BlockSpec((1, tg, G_WIDTH), row), pl.BlockSpec((1, tg, G_WIDTH), row),
                  pl.BlockSpec((1, tg, G_QK), row), pl.BlockSpec(gn.shape, lambda bi, ti: (0, 0)), st_spec],
        out_specs=[pl.BlockSpec((1, tg, G_WIDTH), row), st_spec],
        out_shape=[jax.ShapeDtypeStruct((b, t, G_WIDTH), BF16), jax.ShapeDtypeStruct((b, G_WIDTH, G_QK), F32)],
        scratch_shapes=[pltpu.VMEM((G_WIDTH, G_QK), F32)],
        compiler_params=_cparams(("arbitrary", "arbitrary")),
        name="gla",
    )(gq, gk, gv, gr, la, gn, st0)


def _merge_kernel(x_ref, ao_ref, go_ref, lo_ref, nm_ref, wm_ref, bm_ref, wba_ref, wbg_ref, wbl_ref, wo_ref, o_ref):
    x = x_ref[...]
    xb = _rms(x, nm_ref[...]).astype(BF16)
    g = _sigmoid(_dot(xb, wm_ref[...]) + bm_ref[...])
    merged = (g[:, :D_MODEL] * _dot(ao_ref[...], wba_ref[...])
              + g[:, D_MODEL:2 * D_MODEL] * _dot(go_ref[...], wbg_ref[...])
              + g[:, 2 * D_MODEL:] * _dot(lo_ref[...], wbl_ref[...]))
    o_ref[...] = x + _dot(merged.astype(BF16), wo_ref[...])


def _merge_call(x2, ao, go, lo, w):
    n, d = x2.shape
    tm = min(n, 256)
    assert n % tm == 0
    row = lambda i: (i, 0)
    weights = (w['norm_mix'], w['w_merge'], w['b_merge'], w['w_branch_attn'], w['w_branch_gla'], w['w_branch_lru'],
               w['w_out'])
    return pl.pallas_call(
        _merge_kernel,
        grid=(n // tm,),
        in_specs=[pl.BlockSpec((tm, d), row), pl.BlockSpec((tm, A_WIDTH), row), pl.BlockSpec((tm, G_WIDTH), row),
                  pl.BlockSpec((tm, LRU_WIDTH), row)] + [pl.BlockSpec(a.shape, lambda i: (0, 0)) for a in weights],
        out_specs=pl.BlockSpec((tm, d), row),
        out_shape=jax.ShapeDtypeStruct((n, d), F32),
        compiler_params=_cparams(("arbitrary",)),
        name="merge",
    )(x2, ao, go, lo, *weights)


def _ffn_kernel(x_ref, nf_ref, wg_ref, cw_ref, cb_ref, wu_ref, wd_ref, nl_ref, buf0_ref, o_ref, bufo_ref, ubuf,
                *, tm, final):
    t = pl.program_id(1)
    x = x_ref[0]
    hb = _rms(x, nf_ref[...]).astype(BF16)

    @pl.when(t == 0)
    def _():
        ubuf[0:CARRY_ROWS] = buf0_ref[0]

    ubuf[CARRY_ROWS:CARRY_ROWS + tm] = _dot(hb, wg_ref[...])
    cw = cw_ref[...]
    gc = cb_ref[...]
    for j in range(FFN_CONV):
        o = CARRY_ROWS - (FFN_CONV - 1) + j
        gc = gc + ubuf[o:o + tm] * cw[j:j + 1]
    tail = ubuf[tm:tm + CARRY_ROWS]
    bufo_ref[0] = tail
    ubuf[0:CARRY_ROWS] = tail
    f = _gelu_tanh(gc) * _dot(hb, wu_ref[...])
    y = x + _dot(f.astype(BF16), wd_ref[...])
    o_ref[0] = _rms(y, nl_ref[...]) if final else y


def _ffn_call(x, w, norm_last, buf0, final):
    b, t, d = x.shape
    tm = min(t, 256)
    assert t % tm == 0
    row = lambda bi, ti: (bi, ti, 0)
    const2 = lambda bi, ti: (0, 0)
    buf_spec = pl.BlockSpec((1, CARRY_ROWS, D_FF), lambda bi, ti: (bi, 0, 0))
    weights = (w['norm_ffn'], w['w_ffn_gate'], w['ffn_conv_w'], w['ffn_conv_b'], w['w_ffn_up'], w['w_ffn_down'],
               norm_last)
    return pl.pallas_call(
        functools.partial(_ffn_kernel, tm=tm, final=final),
        grid=(b, t // tm),
        in_specs=[pl.BlockSpec((1, tm, d), row)] + [pl.BlockSpec(a.shape, const2) for a in weights] + [buf_spec],
        out_specs=[pl.BlockSpec((1, tm, d), row), buf_spec],
        out_shape=[jax.ShapeDtypeStruct((b, t, d), F32), jax.ShapeDtypeStruct((b, CARRY_ROWS, D_FF), F32)],
        scratch_shapes=[pltpu.VMEM((tm + CARRY_ROWS, D_FF), F32)],
        compiler_params=_cparams(("arbitrary", "arbitrary")),
        name="ffn",
    )(x, *weights, buf0)


def _rope_tables(p_len, t):
    half = ROPE_DIM // 2
    pos = (p_len + jnp.arange(t, dtype=jnp.int32)).astype(F32)
    inv = ROPE_THETA ** (-jnp.arange(half, dtype=F32) / half)
    ang = pos[:, None] * inv[None, :]
    cos, sin = jnp.cos(ang), jnp.sin(ang)
    rest = A_HEAD_DIM - ROPE_DIM
    z = jnp.zeros((t, half), F32)
    c64 = jnp.concatenate([cos, cos, jnp.ones((t, rest), F32)], axis=1)
    sa64 = jnp.concatenate([-sin, z, jnp.zeros((t, rest), F32)], axis=1)
    sb64 = jnp.concatenate([z, sin, jnp.zeros((t, rest), F32)], axis=1)
    rep = LANES // A_HEAD_DIM
    return tuple(jnp.tile(a, (1, rep)) for a in (c64, sa64, sb64))


def _block_diag(wb):
    n, bi, bo = wb.shape
    eye = jnp.eye(n, dtype=wb.dtype)
    return (eye[:, None, :, None] * wb[:, :, None, :]).reshape(n * bi, n * bo)


def _prep_layer(l, p):
    aw = 2 * A_HEADS * A_HEAD_DIM
    o_gla = 2 * aw + A_WIDTH
    o_ga = o_gla + 2 * G_QK + 2 * G_WIDTH
    o_lru = o_ga + G_GATE_RANK
    w_in = p['w_in'][l]
    row = lambda a: a[l].reshape(1, -1)
    return {
        'norm_mix': row(p['norm_mix']),
        'wqkv': w_in[:, :o_gla].astype(BF16),
        'wgla': w_in[:, o_gla:o_ga].astype(BF16),
        'wga': w_in[:, o_ga:o_lru].astype(BF16),
        'wlru': w_in[:, o_lru:].astype(BF16),
        'wg2': p['w_gla_gate2'][l].astype(BF16),
        'bg2': row(p['b_gla_gate']),
        'lambda_qk': p['lambda_qk'][l],
        'attn_subln': row(p['attn_subln']),
        'gla_norm': row(p['gla_norm']),
        'lru_conv_w': p['lru_conv_w'][l],
        'lru_conv_b': row(p['lru_conv_b']),
        'wa_bd': _block_diag(p['lru_wa'][l]).astype(BF16),
        'lru_ba': row(p['lru_ba']),
        'wx_bd': _block_diag(p['lru_wx'][l]).astype(BF16),
        'lru_bx': row(p['lru_bx']),
        'lru_lambda': row(p['lru_lambda']),
        'w_branch_attn': p['w_branch_attn'][l].astype(BF16),
        'w_branch_gla': p['w_branch_gla'][l].astype(BF16),
        'w_branch_lru': p['w_branch_lru'][l].astype(BF16),
        'w_merge': p['w_merge'][l].astype(BF16),
        'b_merge': row(p['b_merge']),
        'w_out': p['w_out'][l].astype(BF16),
        'norm_ffn': row(p['norm_ffn']),
        'w_ffn_gate': p['w_ffn_gate'][l].astype(BF16),
        'ffn_conv_w': p['ffn_conv_w'][l],
        'ffn_conv_b': row(p['ffn_conv_b']),
        'w_ffn_up': p['w_ffn_up'][l].astype(BF16),
        'w_ffn_down': p['w_ffn_down'][l].astype(BF16),
    }


def _pad_carry(buf):
    return jnp.pad(buf, ((0, 0), (CARRY_ROWS - buf.shape[1], 0), (0, 0)))


def _state_to_kernel(s):
    b = s.shape[0]
    eye = jnp.eye(G_HEADS, dtype=s.dtype)
    st = jnp.swapaxes(s, 2, 3)
    return (st[:, :, :, None, :] * eye[None, :, None, :, None]).reshape(b, G_WIDTH, G_QK)


def _state_from_kernel(st):
    b = st.shape[0]
    s5 = st.reshape(b, G_HEADS, G_DV, G_HEADS, G_DK)
    diag = jnp.stack([s5[:, h, :, h, :] for h in range(G_HEADS)], axis=1)
    return jnp.swapaxes(diag, 2, 3)


def _trunk(x, caches, layers, norm_final):
    cache_k, cache_v, st_gla, st_lconv, st_lh, st_fconv = caches
    b, t, d = x.shape
    p_len = 0 if cache_k is None else cache_k.shape[2]
    tabs = _rope_tables(p_len, t)
    outs = [[], [], [], [], [], []]
    for l, w in enumerate(layers):
        lam_init = 0.8 - 0.6 * math.exp(-0.3 * l)
        (kf, vf, qb, kb, vt, gq, gk, gv, gr, la, lo, lbuf, hl) = _proj_call(
            x, tabs, w, _pad_carry(st_lconv[l]), st_lh[l].reshape(b, 1, LRU_WIDTH))
        if cache_k is None:
            ao = _attn_prompt_call(w['lambda_qk'], w['attn_subln'], qb, kb, vt, lam_init)
        else:
            ao = _attn_cached_call(w['lambda_qk'], w['attn_subln'], qb, kf, vf,
                                   cache_k[l].reshape(b, p_len, -1), cache_v[l].reshape(b, p_len, -1), lam_init)
        go, st_new = _gla_call(gq, gk, gv, gr, la, w['gla_norm'], _state_to_kernel(st_gla[l]))
        x1 = _merge_call(x.reshape(b * t, d), ao.reshape(b * t, -1), go.reshape(b * t, -1), lo.reshape(b * t, -1), w)
        x, fbuf = _ffn_call(x1.reshape(b, t, d), w, norm_final.reshape(1, -1), _pad_carry(st_fconv[l]),
                            final=(l == len(layers) - 1))
        outs[0].append(kf.reshape(b, t, A_HEADS, 2, A_HEAD_DIM))
        outs[1].append(vf.reshape(b, t, A_HEADS, A_VDIM))
        outs[2].append(_state_from_kernel(st_new))
        outs[3].append(lbuf[:, CARRY_ROWS - (LRU_CONV - 1):])
        outs[4].append(hl.reshape(b, LRU_WIDTH))
        outs[5].append(fbuf[:, CARRY_ROWS - (FFN_CONV - 1):])
    return x, [jnp.stack(o) for o in outs]


@jax.jit
def _forward(x_prompt, x_sample, cache_attn_k, cache_attn_v, state_gla, state_lru_conv, state_lru_h, state_ffn_conv,
             params, norm_final):
    layers = [_prep_layer(l, params) for l in range(DEPTH)]
    bp = x_prompt.shape[0]
    zeros = (None, None,
             jnp.zeros((DEPTH, bp, G_HEADS, G_DK, G_DV), F32),
             jnp.zeros((DEPTH, bp, LRU_CONV - 1, LRU_WIDTH), F32),
             jnp.zeros((DEPTH, bp, LRU_WIDTH), F32),
             jnp.zeros((DEPTH, bp, FFN_CONV - 1, D_FF), F32))
    y_p, new_p = _trunk(x_prompt, zeros, layers, norm_final)
    y_s, new_s = _trunk(x_sample, (cache_attn_k, cache_attn_v, state_gla, state_lru_conv, state_lru_h,
                                   state_ffn_conv), layers, norm_final)
    return (y_p, y_s, *new_p, *new_s)


def kernel(x_prompt, x_sample, cache_attn_k, cache_attn_v, state_gla, state_lru_conv, state_lru_h, state_ffn_conv, norm_mix, w_in, lambda_qk, attn_subln, w_gla_gate2, b_gla_gate, gla_norm, lru_conv_w, lru_conv_b, lru_wa, lru_ba, lru_wx, lru_bx, lru_lambda, w_branch_attn, w_branch_gla, w_branch_lru, w_merge, b_merge, w_out, norm_ffn, w_ffn_gate, ffn_conv_w, ffn_conv_b, w_ffn_up, w_ffn_down, norm_final):
    params = dict(norm_mix=norm_mix, w_in=w_in, lambda_qk=lambda_qk, attn_subln=attn_subln, w_gla_gate2=w_gla_gate2,
                  b_gla_gate=b_gla_gate, gla_norm=gla_norm, lru_conv_w=lru_conv_w, lru_conv_b=lru_conv_b,
                  lru_wa=lru_wa, lru_ba=lru_ba, lru_wx=lru_wx, lru_bx=lru_bx, lru_lambda=lru_lambda,
                  w_branch_attn=w_branch_attn, w_branch_gla=w_branch_gla, w_branch_lru=w_branch_lru,
                  w_merge=w_merge, b_merge=b_merge, w_out=w_out, norm_ffn=norm_ffn, w_ffn_gate=w_ffn_gate,
                  ffn_conv_w=ffn_conv_w, ffn_conv_b=ffn_conv_b, w_ffn_up=w_ffn_up, w_ffn_down=w_ffn_down)
    return _forward(x_prompt, x_sample, cache_attn_k, cache_attn_v, state_gla, state_lru_conv, state_lru_h,
                    state_ffn_conv, params, norm_final)
```

```python
import functools
import math

import jax
import jax.numpy as jnp
from jax import lax
from jax.experimental import pallas as pl
from jax.experimental.pallas import tpu as pltpu

F32 = jnp.float32
BF16 = jnp.bfloat16

D_MODEL = 1024
DEPTH = 4
CHUNK = 64
EPS = 1e-6
A_HEADS = 4
A_HEAD_DIM = 64
A_VDIM = 128
A_WIDTH = A_HEADS * A_VDIM
ROPE_DIM = 16
ROPE_THETA = 500000.0
G_HEADS = 4
G_DK = 64
G_DV = 128
G_QK = G_HEADS * G_DK
G_WIDTH = G_HEADS * G_DV
G_GATE_RANK = 16
G_GATE_TAU = 16.0
LRU_WIDTH = 512
LRU_BLOCKS = 8
LRU_CONV = 4
LRU_C = 8.0
D_FF = 2816
FFN_CONV = 3

LANES = 128
SUBLANES = 8
BF16_SUBLANES = 16
LOG2E = math.log2(math.e)
CARRY_ROWS = 8
NEG_BIG = -1e30
VMEM_LIMIT = 56 * 1024 * 1024


def _cparams(sem):
    return pltpu.CompilerParams(dimension_semantics=sem, vmem_limit_bytes=VMEM_LIMIT)


def _rms(x, g):
    return x * lax.rsqrt(jnp.mean(x * x, axis=-1, keepdims=True) + EPS) * g


def _sigmoid(x):
    return 0.5 * jnp.tanh(0.5 * x) + 0.5


def _sqrt_bounded(x):
    return jnp.where(x == 0.0, 0.0, x * lax.rsqrt(x))


def _softplus(x):
    return jnp.maximum(x, 0.0) + jnp.log1p(jnp.exp(-jnp.abs(x)))


def _gelu_tanh(x):
    return x * (0.5 * (1.0 + jnp.tanh(math.sqrt(2.0 / math.pi) * (x + 0.044715 * (x * x * x)))))


def _dot(a, b):
    return jnp.dot(a, b, preferred_element_type=F32)


def _dot_nt(a, b):
    return lax.dot_general(a, b, (((1,), (1,)), ((), ())), preferred_element_type=F32)


def _dot_tn(a, b):
    return lax.dot_general(a, b, (((0,), (0,)), ((), ())), preferred_element_type=F32)


def _lane_tile(x, n):
    return x if n == 1 else jnp.concatenate([x] * n, axis=1)


def _rope(x, c, sa, sb):
    segs = []
    for g in range(x.shape[1] // LANES):
        seg = x[:, g * LANES:(g + 1) * LANES]
        seg_up = pltpu.roll(seg, LANES - ROPE_DIM // 2, 1)
        seg_dn = pltpu.roll(seg, ROPE_DIM // 2, 1)
        segs.append(seg * c + seg_up * sa + seg_dn * sb)
    return jnp.concatenate(segs, axis=1)


def _causal_conv(x, prev, w, bias):
    tm, width = x.shape
    taps = w.shape[0]
    groups = tm // SUBLANES
    x3 = jnp.concatenate([prev.reshape(1, SUBLANES, width), x.reshape(groups, SUBLANES, width)], axis=0)
    sub = lax.broadcasted_iota(jnp.int32, (groups, SUBLANES, width), 1)
    y = bias + x3[1:] * w[taps - 1:taps]
    for d in range(1, taps):
        r = pltpu.roll(x3, d, 1)
        y = y + jnp.where(sub >= d, r[1:], r[:-1]) * w[taps - 1 - d:taps - d]
    return y.reshape(tm, width)


def _lin_scan(a, u, h_in, h_ref):
    tm, width = a.shape
    groups = tm // SUBLANES
    a = a.reshape(groups, SUBLANES, width)
    u = u.reshape(groups, SUBLANES, width)
    sub = lax.broadcasted_iota(jnp.int32, a.shape, 1)
    d = 1
    while d < SUBLANES:
        a_s = pltpu.roll(a, d, 1)
        u_s = pltpu.roll(u, d, 1)
        valid = sub >= d
        u = jnp.where(valid, a * u_s + u, u)
        a = jnp.where(valid, a * a_s, a)
        d *= 2
    carry = h_in
    for g in range(groups):
        h_ref[g * SUBLANES:(g + 1) * SUBLANES] = a[g] * carry + u[g]
        carry = h_ref[(g + 1) * SUBLANES - 1:(g + 1) * SUBLANES]


def _proj_kernel(x_ref, cos_ref, sa_ref, sb_ref, nm_ref, wqkv_ref, wgla_ref, wga_ref, wg2_ref, bg2_ref, wlru_ref,
                 cw_ref, cb_ref, wa_ref, ba_ref, wx_ref, bx_ref, lam_ref, buf0_ref, h0_ref,
                 kt_ref, vr_ref, qb_ref, kb_ref, vt_ref, gq_ref, gk_ref, gv_ref, gr_ref, la_ref, lo_ref,
                 bufo_ref, hl_ref, xbuf, hc, hbuf, *, tm):
    t = pl.program_id(1)
    xn = _rms(x_ref[0], nm_ref[...])
    xb = xn.astype(BF16)

    lr = _dot(xb, wlru_ref[...])
    lx = lr[:, :LRU_WIDTH]
    lg = lr[:, LRU_WIDTH:]

    @pl.when(t == 0)
    def _():
        xbuf[...] = buf0_ref[0]
        hc[...] = h0_ref[0]

    xc = _causal_conv(lx, xbuf[...], cw_ref[...], cb_ref[...])
    tail = lx[tm - CARRY_ROWS:]
    bufo_ref[0] = tail
    xbuf[...] = tail

    xcb = xc.astype(BF16)
    r_pre = _dot(xcb, wa_ref[...]) + ba_ref[...]
    i_pre = _dot(xcb, wx_ref[...]) + bx_ref[...]

    qkv = _dot(xb, wqkv_ref[...])
    c, sa, sb = cos_ref[...], sa_ref[...], sb_ref[...]
    aw = 2 * A_HEADS * A_HEAD_DIM
    q = _rope(qkv[:, :aw], c, sa, sb) * (A_HEAD_DIM ** -0.5 * LOG2E)
    k = _rope(qkv[:, aw:2 * aw], c, sa, sb)
    v = qkv[:, 2 * aw:]
    kt_ref[0, 0] = k.T
    for h in range(A_HEADS):
        vr_ref[0, 0, pl.ds(h, tm, stride=A_HEADS), :] = v[:, h * A_VDIM:(h + 1) * A_VDIM]
    kb_ref[0] = k.astype(BF16)
    vt_ref[0, :, 0] = v.T.reshape(A_HEADS, A_VDIM, tm).astype(BF16)
    lane = lax.broadcasted_iota(jnp.int32, (tm, LANES), 1)
    for h in range(A_HEADS):
        seg = q[:, h * LANES:(h + 1) * LANES]
        qb_ref[0, :, 2 * h * LANES:(2 * h + 1) * LANES] = jnp.where(lane < A_HEAD_DIM, seg, 0.0).astype(BF16)
        qb_ref[0, :, (2 * h + 1) * LANES:(2 * h + 2) * LANES] = jnp.where(lane >= A_HEAD_DIM, seg, 0.0).astype(BF16)

    g = _dot(xb, wgla_ref[...])
    gq_ref[0] = g[:, :G_QK] * (G_DK ** -0.5)
    gk_ref[0] = g[:, G_QK:2 * G_QK]
    gv_ref[0] = g[:, 2 * G_QK:2 * G_QK + G_WIDTH].astype(BF16)
    gr_ref[0] = g[:, 2 * G_QK + G_WIDTH:]
    ga = _dot(xb, wga_ref[...])
    gate = _dot(ga.astype(BF16), wg2_ref[...]) + bg2_ref[...]
    la_ref[0] = -_softplus(-gate) * (1.0 / G_GATE_TAU)

    log_a = (-LRU_C) * _sigmoid(r_pre) * _softplus(-lam_ref[...])
    a = jnp.exp(log_a)
    u = _sqrt_bounded(-jnp.tanh(log_a) * (a * a + 1.0)) * (_sigmoid(i_pre) * xc)
    _lin_scan(a, u, hc[...], hbuf)
    h_last = hbuf[tm - 1:tm]
    hc[...] = h_last
    hl_ref[0] = h_last
    lo_ref[0] = (hbuf[...] * _gelu_tanh(lg)).astype(BF16)


def _proj_kernel_stacked(kst_ref, vst_ref, *refs, tm):
    del kst_ref, vst_ref
    _proj_kernel(*refs, tm=tm)


def _proj_call(x, tabs, w, buf0, h0, layer, depth, stacks):
    b, t, d = x.shape
    tm = min(t, 256)
    assert t % tm == 0 and tm % 8 == 0
    nt = t // tm
    row = lambda bi, ti: (bi, ti, 0)
    const2 = lambda bi, ti: (0, 0)
    tab_spec = pl.BlockSpec((tm, LANES), lambda bi, ti: (ti, 0))

    def wspec(a):
        return pl.BlockSpec(a.shape, const2)

    weights = (w['norm_mix'], w['wqkv'], w['wgla'], w['wga'], w['wg2'], w['bg2'], w['wlru'], w['lru_conv_w'],
               w['lru_conv_b'], w['wa_bd'], w['lru_ba'], w['wx_bd'], w['lru_bx'], w['lru_lambda'])
    in_specs = ([pl.BlockSpec((1, tm, d), row), tab_spec, tab_spec, tab_spec] + [wspec(a) for a in weights]
                + [pl.BlockSpec((1, CARRY_ROWS, LRU_WIDTH), lambda bi, ti: (bi, 0, 0)),
                   pl.BlockSpec((1, 1, LRU_WIDTH), lambda bi, ti: (bi, 0, 0))])
    aw = 2 * A_HEADS * A_HEAD_DIM

    def out(wd, dt):
        return jax.ShapeDtypeStruct((b, t, wd), dt), pl.BlockSpec((1, tm, wd), row)

    vt_out = (jax.ShapeDtypeStruct((b, A_HEADS, nt, A_VDIM, tm), BF16),
              pl.BlockSpec((1, A_HEADS, 1, A_VDIM, tm), lambda bi, ti: (bi, 0, ti, 0, 0)))
    kt_out = (jax.ShapeDtypeStruct((depth, b, aw, t), F32),
              pl.BlockSpec((1, 1, aw, tm), lambda bi, ti: (layer, bi, 0, ti)))
    vr_out = (jax.ShapeDtypeStruct((depth, b, t * A_HEADS, A_VDIM), F32),
              pl.BlockSpec((1, 1, tm * A_HEADS, A_VDIM), lambda bi, ti: (layer, bi, ti, 0)))
    outs = [kt_out, vr_out, out(2 * aw, BF16), out(aw, BF16), vt_out,
            out(G_QK, F32), out(G_QK, F32), out(G_WIDTH, BF16), out(G_WIDTH, F32), out(G_QK, F32),
            out(LRU_WIDTH, BF16),
            (jax.ShapeDtypeStruct((b, CARRY_ROWS, LRU_WIDTH), F32),
             pl.BlockSpec((1, CARRY_ROWS, LRU_WIDTH), lambda bi, ti: (bi, 0, 0))),
            (jax.ShapeDtypeStruct((b, 1, LRU_WIDTH), F32), pl.BlockSpec((1, 1, LRU_WIDTH), lambda bi, ti: (bi, 0, 0)))]
    args = (x, *tabs, *weights, buf0, h0)
    body, aliases = _proj_kernel, {}
    if stacks is not None:
        args = (*stacks, *args)
        in_specs = [pl.BlockSpec(memory_space=pl.ANY)] * 2 + in_specs
        body, aliases = _proj_kernel_stacked, {0: 0, 1: 1}
    return pl.pallas_call(
        functools.partial(body, tm=tm),
        grid=(b, nt),
        in_specs=in_specs,
        out_specs=[o[1] for o in outs],
        out_shape=[o[0] for o in outs],
        input_output_aliases=aliases,
        scratch_shapes=[pltpu.VMEM((CARRY_ROWS, LRU_WIDTH), F32), pltpu.VMEM((1, LRU_WIDTH), F32),
                        pltpu.VMEM((tm, LRU_WIDTH), F32)],
        compiler_params=_cparams(("arbitrary", "arbitrary")),
        name="proj",
    )(*args)


def _attn_lambda(lq_ref, lam_init):
    lq = lq_ref[...]
    s1 = jnp.sum(lq[0:1] * lq[1:2], axis=1, keepdims=True)
    s2 = jnp.sum(lq[2:3] * lq[3:4], axis=1, keepdims=True)
    return jnp.exp(s1) - jnp.exp(s2) + lam_init


def _attn_norm(o, sub, lam_init):
    ms = jnp.mean(o * o, axis=-1, keepdims=True)
    return (o * lax.rsqrt(ms + EPS) * sub * (1.0 - lam_init)).astype(BF16)


def _chunk_mask(q0, k0, tq, tk, keys_on_rows=False):
    shape = (tk, tq) if keys_on_rows else (tq, tk)
    qpos = q0 + lax.broadcasted_iota(jnp.int32, shape, 1 if keys_on_rows else 0)
    kpos = k0 + lax.broadcasted_iota(jnp.int32, shape, 0 if keys_on_rows else 1)
    return (kpos // CHUNK) <= (qpos // CHUNK)


def _attn_prompt_kernel(lq_ref, sub_ref, q_ref, k_ref, vt_ref, o_ref, m_sc, acc_sc, s_sc, mt_sc,
                        *, tq, tk, tv, lam_init):
    i = pl.program_id(2)
    m_sc[...] = jnp.full(m_sc.shape, NEG_BIG, F32)
    acc_sc[...] = jnp.zeros(acc_sc.shape, F32)
    q = q_ref[0]
    qs = (q[:, :LANES], q[:, LANES:])
    ones = jnp.ones((BF16_SUBLANES, tk), BF16)

    def scores(t, slot, masked):
        k = k_ref[0, pl.ds(pl.multiple_of(t * tk, tk), tk), :]
        for c in range(2):
            s = _dot_nt(k, qs[c])
            if masked:
                s = jnp.where(_chunk_mask(i * tq, t * tk, tq, tk, keys_on_rows=True), s, NEG_BIG)
            s_sc[slot, c] = s
            mt_sc[slot, c] = jnp.max(s, axis=0, keepdims=True)

    def consume(t, slot):
        vt = [vt_ref[0, 0, t * (tk // tv) + u] for u in range(tk // tv)]
        vt_aug = jnp.concatenate([_lane_cat(vt), ones], axis=0)
        for c in range(2):
            m_prev = m_sc[c]
            m_new = jnp.maximum(m_prev, mt_sc[slot, c])
            alpha = jnp.exp2(m_prev - m_new)
            p = jnp.exp2(s_sc[slot, c] - m_new).astype(BF16)
            acc_sc[c] = acc_sc[c] * alpha + _dot(vt_aug, p)
            m_sc[c] = m_new

    n_full = i
    n_plain = jnp.maximum(n_full - 1, 0)
    scores(0, 0, True)

    def pair(p, carry):
        t = 2 * p
        scores(t + 1, 1, False)
        consume(t, 0)
        scores(t + 2, 0, False)
        consume(t + 1, 1)
        return carry

    lax.fori_loop(0, n_plain // 2, pair, 0)
    tc = (n_plain // 2) * 2

    @pl.when(n_full == 0)
    def _():
        consume(0, 0)

    @pl.when(jnp.logical_and(n_full > 0, n_plain % 2 == 0))
    def _():
        scores(tc + 1, 1, True)
        consume(tc, 0)
        consume(tc + 1, 1)

    @pl.when(jnp.logical_and(n_full > 0, n_plain % 2 == 1))
    def _():
        scores(tc + 1, 1, False)
        consume(tc, 0)
        scores(tc + 2, 0, True)
        consume(tc + 1, 1)
        consume(tc + 2, 0)

    a0, a1 = acc_sc[0], acc_sc[1]
    lam = _attn_lambda(lq_ref, lam_init)
    o_t = a0[:A_VDIM] / a0[A_VDIM:A_VDIM + 1] - lam * (a1[:A_VDIM] / a1[A_VDIM:A_VDIM + 1])
    o_ref[0] = _attn_norm(o_t.T, sub_ref[...], lam_init)


def _lane_cat(parts):
    return parts[0] if len(parts) == 1 else jnp.concatenate(parts, axis=1)


def _attn_prompt_call(lq, sub, qb, kb, vt, lam_init):
    b, t, _ = kb.shape
    tv = vt.shape[-1]
    tq = min(t, 512)
    tk = tq
    assert t % tq == 0 and (tq % CHUNK == 0 or tq == t) and tq % LANES == 0 and tk % tv == 0
    return pl.pallas_call(
        functools.partial(_attn_prompt_kernel, tq=tq, tk=tk, tv=tv, lam_init=lam_init),
        grid=(b, A_HEADS, t // tq),
        in_specs=[pl.BlockSpec(lq.shape, lambda bi, h, i: (0, 0)),
                  pl.BlockSpec(sub.shape, lambda bi, h, i: (0, 0)),
                  pl.BlockSpec((1, tq, 2 * LANES), lambda bi, h, i: (bi, i, h)),
                  pl.BlockSpec((1, t, LANES), lambda bi, h, i: (bi, 0, h)),
                  pl.BlockSpec((1, 1, t // tv, A_VDIM, tv), lambda bi, h, i: (bi, h, 0, 0, 0))],
        out_specs=pl.BlockSpec((1, tq, LANES), lambda bi, h, i: (bi, i, h)),
        out_shape=jax.ShapeDtypeStruct((b, t, A_WIDTH), BF16),
        scratch_shapes=[pltpu.VMEM((2, 1, tq), F32), pltpu.VMEM((2, A_VDIM + BF16_SUBLANES, tq), F32),
                        pltpu.VMEM((2, 2, tk, tq), F32), pltpu.VMEM((2, 2, 1, tq), F32)],
        compiler_params=_cparams(("arbitrary", "arbitrary", "arbitrary")),
        name="attn_prompt",
    )(lq, sub, qb, kb, vt)


def _attn_cached_kernel(lq_ref, sub_ref, q_ref, kpt_ref, vp_ref, kn_ref, vn_ref, o_ref, *, t, p_len, lam_init):
    mask_p = _chunk_mask(p_len, 0, t, p_len)
    mask_n = _chunk_mask(p_len, p_len, t, t)
    lam = _attn_lambda(lq_ref, lam_init)
    ones_p = jnp.ones((p_len, LANES), BF16)
    ones_n = jnp.ones((t, LANES), BF16)
    for h in range(A_HEADS):
        cols = slice(h * LANES, (h + 1) * LANES)
        kpt = kpt_ref[0, 0, cols, :].astype(BF16)
        kn = kn_ref[0, :, cols]
        vp = jnp.concatenate([vp_ref[0, 0, pl.ds(h, p_len, stride=A_HEADS), :].astype(BF16), ones_p], axis=1)
        vn = jnp.concatenate([vn_ref[0, 0, pl.ds(h, t, stride=A_HEADS), :].astype(BF16), ones_n], axis=1)
        accs = []
        for c in range(2):
            qc = q_ref[0, :, (2 * h + c) * LANES:(2 * h + c + 1) * LANES]
            s_p = jnp.where(mask_p, _dot(qc, kpt), NEG_BIG)
            s_n = jnp.where(mask_n, _dot_nt(qc, kn), NEG_BIG)
            m = jnp.maximum(jnp.max(s_p, axis=1, keepdims=True), jnp.max(s_n, axis=1, keepdims=True))
            accs.append(_dot(jnp.exp2(s_p - m).astype(BF16), vp) + _dot(jnp.exp2(s_n - m).astype(BF16), vn))
        a0, a1 = accs
        o = a0[:, :LANES] / a0[:, LANES:] - lam * (a1[:, :LANES] / a1[:, LANES:])
        o_ref[0, :, cols] = _attn_norm(o, sub_ref[...], lam_init)


def _attn_cached_call(lq, sub, qb, k_new, v_new_rows, k_past_t, v_past_rows, layer, lam_init):
    b, t, _ = k_new.shape
    p_len = k_past_t.shape[-1]
    aw = 2 * A_HEADS * A_HEAD_DIM
    return pl.pallas_call(
        functools.partial(_attn_cached_kernel, t=t, p_len=p_len, lam_init=lam_init),
        grid=(b,),
        in_specs=[pl.BlockSpec(lq.shape, lambda bi: (0, 0)),
                  pl.BlockSpec(sub.shape, lambda bi: (0, 0)),
                  pl.BlockSpec((1, t, 2 * aw), lambda bi: (bi, 0, 0)),
                  pl.BlockSpec((1, 1, aw, p_len), lambda bi: (layer, bi, 0, 0)),
                  pl.BlockSpec((1, 1, p_len * A_HEADS, A_VDIM), lambda bi: (layer, bi, 0, 0)),
                  pl.BlockSpec((1, t, aw), lambda bi: (bi, 0, 0)),
                  pl.BlockSpec((1, 1, t * A_HEADS, A_VDIM), lambda bi: (layer, bi, 0, 0))],
        out_specs=pl.BlockSpec((1, t, A_WIDTH), lambda bi: (bi, 0, 0)),
        out_shape=jax.ShapeDtypeStruct((b, t, A_WIDTH), BF16),
        compiler_params=_cparams(("arbitrary",)),
        name="attn_cached",
    )(lq, sub, qb, k_past_t, v_past_rows, k_new, v_new_rows)


def _split3(x):
    hi = x.astype(BF16)
    r1 = x - hi.astype(F32)
    mid = r1.astype(BF16)
    lo = (r1 - mid.astype(F32)).astype(BF16)
    return hi, mid, lo


def _gla_kernel(q_ref, k_ref, v_ref, r_ref, la_ref, gn_ref, st0_ref, o_ref, sto_ref, st, *, tg, lc):
    t = pl.program_id(1)

    @pl.when(t == 0)
    def _():
        st[...] = st0_ref[0]

    rr = lax.broadcasted_iota(jnp.int32, (lc, lc), 0)
    cc = lax.broadcasted_iota(jnp.int32, (lc, lc), 1)
    tri = rr >= cc
    tri_b = jnp.where(tri, 1.0, 0.0).astype(BF16)
    lane_k = lax.broadcasted_iota(jnp.int32, (lc, G_QK), 1) // G_DK
    bd = (lax.broadcasted_iota(jnp.int32, (G_WIDTH, G_QK), 0) // G_DV
          == lax.broadcasted_iota(jnp.int32, (G_WIDTH, G_QK), 1) // G_DK)
    gn = gn_ref[...]

    per_chunk = []
    for ci in range(tg // lc):
        rows = slice(ci * lc, (ci + 1) * lc)
        hi, mid, lo = _split3(la_ref[0, rows, :])
        bc = _dot(tri_b, hi) + _dot(tri_b, mid) + _dot(tri_b, lo)
        b_mid = bc[lc // 2 - 1:lc // 2]
        b_last = bc[lc - 1:lc]
        q = q_ref[0, rows, :]
        k = k_ref[0, rows, :]
        vb = v_ref[0, rows, :]
        q_in = (q * jnp.exp(bc)).astype(BF16)
        q_mid = q * jnp.exp(bc - b_mid)
        k_mid = (k * jnp.exp(b_mid - bc)).astype(BF16)
        k_end = (k * jnp.exp(b_last - bc)).astype(BF16)
        att = []
        for h in range(G_HEADS):
            qh = jnp.where(lane_k == h, q_mid, 0.0).astype(BF16)
            att.append(jnp.where(tri, _dot_nt(qh, k_mid), 0.0).astype(BF16))
        incr = jnp.where(bd, _dot_tn(vb, k_end), 0.0)
        per_chunk.append((rows, q_in, att, vb, incr, jnp.exp(b_last)))

    intras = [[_dot(att[h], vb[:, h * G_DV:(h + 1) * G_DV]) for h in range(G_HEADS)]
              for _, _, att, vb, _, _ in per_chunk]

    s_cur = st[...]
    for (rows, q_in, _, _, incr, decay), intra in zip(per_chunk, intras):
        inter = _dot_nt(q_in, s_cur.astype(BF16))
        outs = []
        for h in range(G_HEADS):
            oh = inter[:, h * G_DV:(h + 1) * G_DV] + intra[h]
            ms = jnp.mean(oh * oh, axis=-1, keepdims=True)
            outs.append(oh * lax.rsqrt(ms + EPS) * gn)
        gr = r_ref[0, rows, :]
        o_ref[0, rows, :] = (jnp.concatenate(outs, axis=1) * (gr * _sigmoid(gr))).astype(BF16)
        s_cur = s_cur * decay + incr
    st[...] = s_cur
    sto_ref[0] = s_cur


def _gla_call(gq, gk, gv, gr, la, gn, st0):
    b, t, _ = gq.shape
    lc = min(CHUNK, t)
    tg = min(t, 512)
    assert t % tg == 0 and tg % lc == 0 and lc % 16 == 0
    row = lambda bi, ti: (bi, ti, 0)
    st_spec = pl.BlockSpec((1, G_WIDTH, G_QK), lambda bi, ti: (bi, 0, 0))
    return pl.pallas_call(
        functools.partial(_gla_kernel, tg=tg, lc=lc),
        grid=(b, t // tg),
        in_specs=[pl.BlockSpec((1, tg, G_QK), row), pl.BlockSpec((1, tg, G_QK), row),
                  pl.BlockSpec((1, tg, G_WIDTH), row), pl.BlockSpec((1, tg, G_WIDTH), row),
                  pl.BlockSpec((1, tg, G_QK), row), pl.BlockSpec(gn.shape, lambda bi, ti: (0, 0)), st_spec],
        out_specs=[pl.BlockSpec((1, tg, G_WIDTH), row), st_spec],
        out_shape=[jax.ShapeDtypeStruct((b, t, G_WIDTH), BF16), jax.ShapeDtypeStruct((b, G_WIDTH, G_QK), F32)],
        scratch_shapes=[pltpu.VMEM((G_WIDTH, G_QK), F32)],
        compiler_params=_cparams(("arbitrary", "arbitrary")),
        name="gla",
    )(gq, gk, gv, gr, la, gn, st0)


def _merge_kernel(x_ref, ao_ref, go_ref, lo_ref, nm_ref, wm_ref, bm_ref, wba_ref, wbg_ref, wbl_ref, wo_ref, o_ref):
    x = x_ref[...]
    xb = _rms(x, nm_ref[...]).astype(BF16)
    g = _sigmoid(_dot(xb, wm_ref[...]) + bm_ref[...])
    merged = (g[:, :D_MODEL] * _dot(ao_ref[...], wba_ref[...])
              + g[:, D_MODEL:2 * D_MODEL] * _dot(go_ref[...], wbg_ref[...])
              + g[:, 2 * D_MODEL:] * _dot(lo_ref[...], wbl_ref[...]))
    o_ref[...] = x + _dot(merged.astype(BF16), wo_ref[...])


def _merge_call(x2, ao, go, lo, w):
    n, d = x2.shape
    tm = min(n, 256)
    assert n % tm == 0
    row = lambda i: (i, 0)
    weights = (w['norm_mix'], w['w_merge'], w['b_merge'], w['w_branch_attn'], w['w_branch_gla'], w['w_branch_lru'],
               w['w_out'])
    return pl.pallas_call(
        _merge_kernel,
        grid=(n // tm,),
        in_specs=[pl.BlockSpec((tm, d), row), pl.BlockSpec((tm, A_WIDTH), row), pl.BlockSpec((tm, G_WIDTH), row),
                  pl.BlockSpec((tm, LRU_WIDTH), row)] + [pl.BlockSpec(a.shape, lambda i: (0, 0)) for a in weights],
        out_specs=pl.BlockSpec((tm, d), row),
        out_shape=jax.ShapeDtypeStruct((n, d), F32),
        compiler_params=_cparams(("arbitrary",)),
        name="merge",
    )(x2, ao, go, lo, *weights)


def _ffn_kernel(x_ref, nf_ref, wg_ref, cw_ref, cb_ref, wu_ref, wd_ref, nl_ref, buf0_ref, o_ref, bufo_ref, ubuf,
                *, tm, final):
    t = pl.program_id(1)
    x = x_ref[0]
    hb = _rms(x, nf_ref[...]).astype(BF16)

    @pl.when(t == 0)
    def _():
        ubuf[...] = buf0_ref[0]

    gu = _dot(hb, wg_ref[...])
    gc = _causal_conv(gu, ubuf[...], cw_ref[...], cb_ref[...])
    tail = gu[tm - CARRY_ROWS:]
    bufo_ref[0] = tail
    ubuf[...] = tail
    f = _gelu_tanh(gc) * _dot(hb, wu_ref[...])
    y = x + _dot(f.astype(BF16), wd_ref[...])
    o_ref[0] = _rms(y, nl_ref[...]) if final else y


def _ffn_call(x, w, norm_last, buf0, final):
    b, t, d = x.shape
    tm = min(t, 256)
    assert t % tm == 0
    row = lambda bi, ti: (bi, ti, 0)
    const2 = lambda bi, ti: (0, 0)
    buf_spec = pl.BlockSpec((1, CARRY_ROWS, D_FF), lambda bi, ti: (bi, 0, 0))
    weights = (w['norm_ffn'], w['w_ffn_gate'], w['ffn_conv_w'], w['ffn_conv_b'], w['w_ffn_up'], w['w_ffn_down'],
               norm_last)
    return pl.pallas_call(
        functools.partial(_ffn_kernel, tm=tm, final=final),
        grid=(b, t // tm),
        in_specs=[pl.BlockSpec((1, tm, d), row)] + [pl.BlockSpec(a.shape, const2) for a in weights] + [buf_spec],
        out_specs=[pl.BlockSpec((1, tm, d), row), buf_spec],
        out_shape=[jax.ShapeDtypeStruct((b, t, d), F32), jax.ShapeDtypeStruct((b, CARRY_ROWS, D_FF), F32)],
        scratch_shapes=[pltpu.VMEM((CARRY_ROWS, D_FF), F32)],
        compiler_params=_cparams(("arbitrary", "arbitrary")),
        name="ffn",
    )(x, *weights, buf0)


def _rope_tables(p_len, t):
    half = ROPE_DIM // 2
    pos = (p_len + jnp.arange(t, dtype=jnp.int32)).astype(F32)
    inv = ROPE_THETA ** (-jnp.arange(half, dtype=F32) / half)
    ang = pos[:, None] * inv[None, :]
    cos, sin = jnp.cos(ang), jnp.sin(ang)
    rest = A_HEAD_DIM - ROPE_DIM
    z = jnp.zeros((t, half), F32)
    c64 = jnp.concatenate([cos, cos, jnp.ones((t, rest), F32)], axis=1)
    sa64 = jnp.concatenate([-sin, z, jnp.zeros((t, rest), F32)], axis=1)
    sb64 = jnp.concatenate([z, sin, jnp.zeros((t, rest), F32)], axis=1)
    rep = LANES // A_HEAD_DIM
    return tuple(jnp.tile(a, (1, rep)) for a in (c64, sa64, sb64))


def _block_diag(wb):
    n, bi, bo = wb.shape
    eye = jnp.eye(n, dtype=wb.dtype)
    return (eye[:, None, :, None] * wb[:, :, None, :]).reshape(n * bi, n * bo)


def _prep_layer(l, p):
    aw = 2 * A_HEADS * A_HEAD_DIM
    o_gla = 2 * aw + A_WIDTH
    o_ga = o_gla + 2 * G_QK + 2 * G_WIDTH
    o_lru = o_ga + G_GATE_RANK
    w_in = p['w_in'][l]
    row = lambda a: a[l].reshape(1, -1)
    return {
        'norm_mix': row(p['norm_mix']),
        'wqkv': w_in[:, :o_gla].astype(BF16),
        'wgla': w_in[:, o_gla:o_ga].astype(BF16),
        'wga': w_in[:, o_ga:o_lru].astype(BF16),
        'wlru': w_in[:, o_lru:].astype(BF16),
        'wg2': p['w_gla_gate2'][l].astype(BF16),
        'bg2': row(p['b_gla_gate']),
        'lambda_qk': p['lambda_qk'][l],
        'attn_subln': row(p['attn_subln']),
        'gla_norm': row(p['gla_norm']),
        'lru_conv_w': p['lru_conv_w'][l],
        'lru_conv_b': row(p['lru_conv_b']),
        'wa_bd': _block_diag(p['lru_wa'][l]).astype(BF16),
        'lru_ba': row(p['lru_ba']),
        'wx_bd': _block_diag(p['lru_wx'][l]).astype(BF16),
        'lru_bx': row(p['lru_bx']),
        'lru_lambda': row(p['lru_lambda']),
        'w_branch_attn': p['w_branch_attn'][l].astype(BF16),
        'w_branch_gla': p['w_branch_gla'][l].astype(BF16),
        'w_branch_lru': p['w_branch_lru'][l].astype(BF16),
        'w_merge': p['w_merge'][l].astype(BF16),
        'b_merge': row(p['b_merge']),
        'w_out': p['w_out'][l].astype(BF16),
        'norm_ffn': row(p['norm_ffn']),
        'w_ffn_gate': p['w_ffn_gate'][l].astype(BF16),
        'ffn_conv_w': p['ffn_conv_w'][l],
        'ffn_conv_b': row(p['ffn_conv_b']),
        'w_ffn_up': p['w_ffn_up'][l].astype(BF16),
        'w_ffn_down': p['w_ffn_down'][l].astype(BF16),
    }


def _pad_carry(buf):
    return jnp.pad(buf, ((0, 0), (CARRY_ROWS - buf.shape[1], 0), (0, 0)))


def _state_to_kernel(s):
    b = s.shape[0]
    eye = jnp.eye(G_HEADS, dtype=s.dtype)
    st = jnp.swapaxes(s, 2, 3)
    return (st[:, :, :, None, :] * eye[None, :, None, :, None]).reshape(b, G_WIDTH, G_QK)


def _state_from_kernel(st):
    b = st.shape[0]
    s5 = st.reshape(b, G_HEADS, G_DV, G_HEADS, G_DK)
    diag = jnp.stack([s5[:, h, :, h, :] for h in range(G_HEADS)], axis=1)
    return jnp.swapaxes(diag, 2, 3)


def _trunk(x, caches, layers, norm_final):
    cache_k, cache_v, st_gla, st_lconv, st_lh, st_fconv = caches
    b, t, d = x.shape
    p_len = 0 if cache_k is None else cache_k.shape[2]
    tabs = _rope_tables(p_len, t)
    if cache_k is not None:
        depth = cache_k.shape[0]
        cache_kt = jnp.transpose(cache_k, (0, 1, 3, 4, 5, 2)).reshape(depth, b, -1, p_len)
        cache_vr = cache_v.reshape(depth, b, p_len * A_HEADS, A_VDIM)
    outs = [[], [], [], [], [], []]
    stacks = None
    for l, w in enumerate(layers):
        lam_init = 0.8 - 0.6 * math.exp(-0.3 * l)
        (kt_all, vr_all, qb, kb, vt, gq, gk, gv, gr, la, lo, lbuf, hl) = _proj_call(
            x, tabs, w, _pad_carry(st_lconv[l]), st_lh[l].reshape(b, 1, LRU_WIDTH), l, len(layers), stacks)
        stacks = (kt_all, vr_all)
        if cache_k is None:
            ao = _attn_prompt_call(w['lambda_qk'], w['attn_subln'], qb, kb, vt, lam_init)
        else:
            ao = _attn_cached_call(w['lambda_qk'], w['attn_subln'], qb, kb, vr_all, cache_kt, cache_vr, l, lam_init)
        go, st_new = _gla_call(gq, gk, gv, gr, la, w['gla_norm'], _state_to_kernel(st_gla[l]))
        x1 = _merge_call(x.reshape(b * t, d), ao.reshape(b * t, -1), go.reshape(b * t, -1), lo.reshape(b * t, -1), w)
        x, fbuf = _ffn_call(x1.reshape(b, t, d), w, norm_final.reshape(1, -1), _pad_carry(st_fconv[l]),
                            final=(l == len(layers) - 1))
        outs[2].append(_state_from_kernel(st_new))
        outs[3].append(lbuf[:, CARRY_ROWS - (LRU_CONV - 1):])
        outs[4].append(hl.reshape(b, LRU_WIDTH))
        outs[5].append(fbuf[:, CARRY_ROWS - (FFN_CONV - 1):])
    kt_all, vr_all = stacks
    depth = len(layers)
    new_k = jnp.transpose(kt_all.reshape(depth, b, A_HEADS, 2, A_HEAD_DIM, t), (0, 1, 5, 2, 3, 4))
    new_v = vr_all.reshape(depth, b, t, A_HEADS, A_VDIM)
    return x, [new_k, new_v] + [jnp.stack(o) for o in outs[2:]]


@jax.jit
def _forward(x_prompt, x_sample, cache_attn_k, cache_attn_v, state_gla, state_lru_conv, state_lru_h, state_ffn_conv,
             params, norm_final):
    layers = [_prep_layer(l, params) for l in range(DEPTH)]
    bp = x_prompt.shape[0]
    zeros = (None, None,
             jnp.zeros((DEPTH, bp, G_HEADS, G_DK, G_DV), F32),
             jnp.zeros((DEPTH, bp, LRU_CONV - 1, LRU_WIDTH), F32),
             jnp.zeros((DEPTH, bp, LRU_WIDTH), F32),
             jnp.zeros((DEPTH, bp, FFN_CONV - 1, D_FF), F32))
    y_p, new_p = _trunk(x_prompt, zeros, layers, norm_final)
    y_s, new_s = _trunk(x_sample, (cache_attn_k, cache_attn_v, state_gla, state_lru_conv, state_lru_h,
                                   state_ffn_conv), layers, norm_final)
    return (y_p, y_s, *new_p, *new_s)


def kernel(x_prompt, x_sample, cache_attn_k, cache_attn_v, state_gla, state_lru_conv, state_lru_h, state_ffn_conv, norm_mix, w_in, lambda_qk, attn_subln, w_gla_gate2, b_gla_gate, gla_norm, lru_conv_w, lru_conv_b, lru_wa, lru_ba, lru_wx, lru_bx, lru_lambda, w_branch_attn, w_branch_gla, w_branch_lru, w_merge, b_merge, w_out, norm_ffn, w_ffn_gate, ffn_conv_w, ffn_conv_b, w_ffn_up, w_ffn_down, norm_final):
    params = dict(norm_mix=norm_mix, w_in=w_in, lambda_qk=lambda_qk, attn_subln=attn_subln, w_gla_gate2=w_gla_gate2,
                  b_gla_gate=b_gla_gate, gla_norm=gla_norm, lru_conv_w=lru_conv_w, lru_conv_b=lru_conv_b,
                  lru_wa=lru_wa, lru_ba=lru_ba, lru_wx=lru_wx, lru_bx=lru_bx, lru_lambda=lru_lambda,
                  w_branch_attn=w_branch_attn, w_branch_gla=w_branch_gla, w_branch_lru=w_branch_lru,
                  w_merge=w_merge, b_merge=b_merge, w_out=w_out, norm_ffn=norm_ffn, w_ffn_gate=w_ffn_gate,
                  ffn_conv_w=ffn_conv_w, ffn_conv_b=ffn_conv_b, w_ffn_up=w_ffn_up, w_ffn_down=w_ffn_down)
    return _forward(x_prompt, x_sample, cache_attn_k, cache_attn_v, state_gla, state_lru_conv, state_lru_h,
                    state_ffn_conv, params, norm_final)
```

```python
import functools
import math

import jax
import jax.numpy as jnp
from jax import lax
from jax.experimental import pallas as pl
from jax.experimental.pallas import tpu as pltpu

F32 = jnp.float32
BF16 = jnp.bfloat16

D_MODEL = 1024
DEPTH = 4
CHUNK = 64
EPS = 1e-6
A_HEADS = 4
A_HEAD_DIM = 64
A_VDIM = 128
A_WIDTH = A_HEADS * A_VDIM
ROPE_DIM = 16
ROPE_THETA = 500000.0
G_HEADS = 4
G_DK = 64
G_DV = 128
G_QK = G_HEADS * G_DK
G_WIDTH = G_HEADS * G_DV
G_GATE_RANK = 16
G_GATE_TAU = 16.0
LRU_WIDTH = 512
LRU_BLOCKS = 8
LRU_CONV = 4
LRU_C = 8.0
D_FF = 2816
FFN_CONV = 3

LANES = 128
SUBLANES = 8
BF16_SUBLANES = 16
LOG2E = math.log2(math.e)
CARRY_ROWS = 8
NEG_BIG = -1e30
VMEM_LIMIT = 56 * 1024 * 1024


def _cparams(sem):
    return pltpu.CompilerParams(dimension_semantics=sem, vmem_limit_bytes=VMEM_LIMIT)


def _rms(x, g):
    return x * lax.rsqrt(jnp.mean(x * x, axis=-1, keepdims=True) + EPS) * g


def _sigmoid(x):
    return 0.5 * jnp.tanh(0.5 * x) + 0.5


def _sqrt_bounded(x):
    return jnp.where(x == 0.0, 0.0, x * lax.rsqrt(x))


def _softplus(x):
    return jnp.maximum(x, 0.0) + jnp.log1p(jnp.exp(-jnp.abs(x)))


def _gelu_tanh(x):
    return x * (0.5 * (1.0 + jnp.tanh(math.sqrt(2.0 / math.pi) * (x + 0.044715 * (x * x * x)))))


def _dot(a, b):
    return jnp.dot(a, b, preferred_element_type=F32)


def _dot_nt(a, b):
    return lax.dot_general(a, b, (((1,), (1,)), ((), ())), preferred_element_type=F32)


def _dot_tn(a, b):
    return lax.dot_general(a, b, (((0,), (0,)), ((), ())), preferred_element_type=F32)


def _lane_tile(x, n):
    return x if n == 1 else jnp.concatenate([x] * n, axis=1)


def _rope(x, c, sa, sb):
    segs = []
    for g in range(x.shape[1] // LANES):
        seg = x[:, g * LANES:(g + 1) * LANES]
        seg_up = pltpu.roll(seg, LANES - ROPE_DIM // 2, 1)
        seg_dn = pltpu.roll(seg, ROPE_DIM // 2, 1)
        segs.append(seg * c + seg_up * sa + seg_dn * sb)
    return jnp.concatenate(segs, axis=1)


def _causal_conv(x, prev, w, bias):
    seqs = prev.shape[0]
    rows, width = x.shape
    taps = w.shape[0]
    gps = rows // seqs // SUBLANES
    x3 = x.reshape(seqs * gps, SUBLANES, width)
    ext = []
    for s in range(seqs):
        ext += [prev[s:s + 1], x3[s * gps:(s + 1) * gps]]
    ext = jnp.concatenate(ext, axis=0)

    def pick(r, first):
        parts = [r[s * (gps + 1) + first:s * (gps + 1) + first + gps] for s in range(seqs)]
        return parts[0] if seqs == 1 else jnp.concatenate(parts, axis=0)

    sub = lax.broadcasted_iota(jnp.int32, x3.shape, 1)
    y = bias + x3 * w[taps - 1:taps]
    for d in range(1, taps):
        r = pltpu.roll(ext, d, 1)
        y = y + jnp.where(sub >= d, pick(r, 1), pick(r, 0)) * w[taps - 1 - d:taps - d]
    return y.reshape(rows, width)


def _last_groups(x, seqs):
    per = x.shape[0] // seqs
    tails = [x[(s + 1) * per - SUBLANES:(s + 1) * per] for s in range(seqs)]
    return jnp.stack(tails, axis=0)


def _lin_scan(a, u, h_in, h_ref):
    tm, width = a.shape
    groups = tm // SUBLANES
    a = a.reshape(groups, SUBLANES, width)
    u = u.reshape(groups, SUBLANES, width)
    sub = lax.broadcasted_iota(jnp.int32, a.shape, 1)
    d = 1
    while d < SUBLANES:
        a_s = pltpu.roll(a, d, 1)
        u_s = pltpu.roll(u, d, 1)
        valid = sub >= d
        u = jnp.where(valid, a * u_s + u, u)
        a = jnp.where(valid, a * a_s, a)
        d *= 2
    carry = h_in
    for g in range(groups):
        h_ref[g * SUBLANES:(g + 1) * SUBLANES] = a[g] * carry + u[g]
        carry = h_ref[(g + 1) * SUBLANES - 1:(g + 1) * SUBLANES]


def _proj_kernel(x_ref, cos_ref, sa_ref, sb_ref, nm_ref, wqkv_ref, wgla_ref, wga_ref, wg2_ref, bg2_ref, wlru_ref,
                 cw_ref, cb_ref, wa_ref, ba_ref, wx_ref, bx_ref, lam_ref, buf0_ref, h0_ref,
                 kt_ref, vr_ref, qb_ref, kb_ref, vt_ref, gq_ref, gk_ref, gv_ref, gr_ref, la_ref, lo_ref,
                 bufo_ref, hl_ref, xbuf, hc, hbuf, *, tm):
    t = pl.program_id(1)
    xn = _rms(x_ref[0], nm_ref[...])
    xb = xn.astype(BF16)

    lx = _dot(xb, wlru_ref[:, :LRU_WIDTH])
    lg = _dot(xb, wlru_ref[:, LRU_WIDTH:])

    @pl.when(t == 0)
    def _():
        xbuf[...] = buf0_ref[...]
        hc[...] = h0_ref[0]

    xc = _causal_conv(lx, xbuf[...], cw_ref[...], cb_ref[...])
    tail = _last_groups(lx, 1)
    bufo_ref[...] = tail
    xbuf[...] = tail

    xcb = xc.astype(BF16)
    r_pre = _dot(xcb, wa_ref[...]) + ba_ref[...]
    i_pre = _dot(xcb, wx_ref[...]) + bx_ref[...]

    c, sa, sb = cos_ref[...], sa_ref[...], sb_ref[...]
    aw = 2 * A_HEADS * A_HEAD_DIM
    q = _rope(_dot(xb, wqkv_ref[:, :aw]), c, sa, sb) * (A_HEAD_DIM ** -0.5 * LOG2E)
    k = _rope(_dot(xb, wqkv_ref[:, aw:2 * aw]), c, sa, sb)
    v = _dot(xb, wqkv_ref[:, 2 * aw:])
    kt_ref[0, 0] = k.T
    for h in range(A_HEADS):
        vr_ref[0, 0, pl.ds(h, tm, stride=A_HEADS), :] = v[:, h * A_VDIM:(h + 1) * A_VDIM]
    kb_ref[0] = k.astype(BF16)
    vt_ref[0, :, 0] = v.T.reshape(A_HEADS, A_VDIM, tm).astype(BF16)
    lane = lax.broadcasted_iota(jnp.int32, (tm, LANES), 1)
    for h in range(A_HEADS):
        seg = q[:, h * LANES:(h + 1) * LANES]
        qb_ref[0, :, 2 * h * LANES:(2 * h + 1) * LANES] = jnp.where(lane < A_HEAD_DIM, seg, 0.0).astype(BF16)
        qb_ref[0, :, (2 * h + 1) * LANES:(2 * h + 2) * LANES] = jnp.where(lane >= A_HEAD_DIM, seg, 0.0).astype(BF16)

    gq_ref[0] = _dot(xb, wgla_ref[:, :G_QK]) * (G_DK ** -0.5)
    gk_ref[0] = _dot(xb, wgla_ref[:, G_QK:2 * G_QK])
    gv_ref[0] = _dot(xb, wgla_ref[:, 2 * G_QK:2 * G_QK + G_WIDTH]).astype(BF16)
    gr_ref[0] = _dot(xb, wgla_ref[:, 2 * G_QK + G_WIDTH:])
    ga = _dot(xb, wga_ref[...])
    gate = _dot(ga.astype(BF16), wg2_ref[...]) + bg2_ref[...]
    la_ref[0] = -_softplus(-gate) * (1.0 / G_GATE_TAU)

    log_a = (-LRU_C) * _sigmoid(r_pre) * _softplus(-lam_ref[...])
    a = jnp.exp(log_a)
    u = _sqrt_bounded(-jnp.tanh(log_a) * (a * a + 1.0)) * (_sigmoid(i_pre) * xc)
    _lin_scan(a, u, hc[...], hbuf)
    h_last = hbuf[tm - 1:tm]
    hc[...] = h_last
    hl_ref[0] = h_last
    lo_ref[0] = (hbuf[...] * _gelu_tanh(lg)).astype(BF16)


def _proj_kernel_stacked(kst_ref, vst_ref, *refs, tm):
    del kst_ref, vst_ref
    _proj_kernel(*refs, tm=tm)


def _proj_call(x, tabs, w, buf0, h0, layer, depth, stacks):
    b, t, d = x.shape
    tm = min(t, 512)
    assert t % tm == 0 and tm % 8 == 0
    nt = t // tm
    row = lambda bi, ti: (bi, ti, 0)
    const2 = lambda bi, ti: (0, 0)
    tab_spec = pl.BlockSpec((tm, LANES), lambda bi, ti: (ti, 0))

    def wspec(a):
        return pl.BlockSpec(a.shape, const2, pipeline_mode=pl.Buffered(1))

    weights = (w['norm_mix'], w['wqkv'], w['wgla'], w['wga'], w['wg2'], w['bg2'], w['wlru'], w['lru_conv_w'],
               w['lru_conv_b'], w['wa_bd'], w['lru_ba'], w['wx_bd'], w['lru_bx'], w['lru_lambda'])
    in_specs = ([pl.BlockSpec((1, tm, d), row), tab_spec, tab_spec, tab_spec] + [wspec(a) for a in weights]
                + [pl.BlockSpec((1, CARRY_ROWS, LRU_WIDTH), lambda bi, ti: (bi, 0, 0)),
                   pl.BlockSpec((1, 1, LRU_WIDTH), lambda bi, ti: (bi, 0, 0))])
    aw = 2 * A_HEADS * A_HEAD_DIM

    def out(wd, dt):
        return jax.ShapeDtypeStruct((b, t, wd), dt), pl.BlockSpec((1, tm, wd), row)

    vt_out = (jax.ShapeDtypeStruct((b, A_HEADS, nt, A_VDIM, tm), BF16),
              pl.BlockSpec((1, A_HEADS, 1, A_VDIM, tm), lambda bi, ti: (bi, 0, ti, 0, 0)))
    kt_out = (jax.ShapeDtypeStruct((depth, b, aw, t), F32),
              pl.BlockSpec((1, 1, aw, tm), lambda bi, ti: (layer, bi, 0, ti)))
    vr_out = (jax.ShapeDtypeStruct((depth, b, t * A_HEADS, A_VDIM), F32),
              pl.BlockSpec((1, 1, tm * A_HEADS, A_VDIM), lambda bi, ti: (layer, bi, ti, 0)))
    outs = [kt_out, vr_out, out(2 * aw, BF16), out(aw, BF16), vt_out,
            out(G_QK, F32), out(G_QK, F32), out(G_WIDTH, BF16), out(G_WIDTH, F32), out(G_QK, F32),
            out(LRU_WIDTH, BF16),
            (jax.ShapeDtypeStruct((b, CARRY_ROWS, LRU_WIDTH), F32),
             pl.BlockSpec((1, CARRY_ROWS, LRU_WIDTH), lambda bi, ti: (bi, 0, 0))),
            (jax.ShapeDtypeStruct((b, 1, LRU_WIDTH), F32), pl.BlockSpec((1, 1, LRU_WIDTH), lambda bi, ti: (bi, 0, 0)))]
    args = (x, *tabs, *weights, buf0, h0)
    body, aliases = _proj_kernel, {}
    if stacks is not None:
        args = (*stacks, *args)
        in_specs = [pl.BlockSpec(memory_space=pl.ANY)] * 2 + in_specs
        body, aliases = _proj_kernel_stacked, {0: 0, 1: 1}
    return pl.pallas_call(
        functools.partial(body, tm=tm),
        grid=(b, nt),
        in_specs=in_specs,
        out_specs=[o[1] for o in outs],
        out_shape=[o[0] for o in outs],
        input_output_aliases=aliases,
        scratch_shapes=[pltpu.VMEM((1, CARRY_ROWS, LRU_WIDTH), F32), pltpu.VMEM((1, LRU_WIDTH), F32),
                        pltpu.VMEM((tm, LRU_WIDTH), F32)],
        compiler_params=_cparams(("arbitrary", "arbitrary")),
        name="proj",
    )(*args)


def _attn_lambda(lq_ref, lam_init):
    lq = lq_ref[...]
    s1 = jnp.sum(lq[0:1] * lq[1:2], axis=1, keepdims=True)
    s2 = jnp.sum(lq[2:3] * lq[3:4], axis=1, keepdims=True)
    return jnp.exp(s1) - jnp.exp(s2) + lam_init


def _attn_norm(o, sub, lam_init):
    ms = jnp.mean(o * o, axis=-1, keepdims=True)
    return (o * lax.rsqrt(ms + EPS) * sub * (1.0 - lam_init)).astype(BF16)


def _chunk_mask(q0, k0, tq, tk, keys_on_rows=False):
    shape = (tk, tq) if keys_on_rows else (tq, tk)
    qpos = q0 + lax.broadcasted_iota(jnp.int32, shape, 1 if keys_on_rows else 0)
    kpos = k0 + lax.broadcasted_iota(jnp.int32, shape, 0 if keys_on_rows else 1)
    return (kpos // CHUNK) <= (qpos // CHUNK)


def _attn_prompt_kernel(lq_ref, sub_ref, q_ref, k_ref, vt_ref, o_ref, m_sc, acc_sc, s_sc, mt_sc,
                        *, tq, tk, tv, lam_init):
    i = pl.program_id(2)
    m_sc[...] = jnp.full(m_sc.shape, NEG_BIG, F32)
    acc_sc[...] = jnp.zeros(acc_sc.shape, F32)
    q = q_ref[0]
    qs = (q[:, :LANES], q[:, LANES:])
    ones = jnp.ones((BF16_SUBLANES, tk), BF16)

    def scores(t, slot, masked):
        k = k_ref[0, pl.ds(pl.multiple_of(t * tk, tk), tk), :]
        for c in range(2):
            s = _dot_nt(k, qs[c])
            if masked:
                s = jnp.where(_chunk_mask(i * tq, t * tk, tq, tk, keys_on_rows=True), s, NEG_BIG)
            s_sc[slot, c] = s
            mt_sc[slot, c] = jnp.max(s, axis=0, keepdims=True)

    def consume(t, slot):
        vt = [vt_ref[0, 0, t * (tk // tv) + u] for u in range(tk // tv)]
        vt_aug = jnp.concatenate([_lane_cat(vt), ones], axis=0)
        for c in range(2):
            m_prev = m_sc[c]
            m_new = jnp.maximum(m_prev, mt_sc[slot, c])
            alpha = jnp.exp2(m_prev - m_new)
            p = jnp.exp2(s_sc[slot, c] - m_new).astype(BF16)
            acc_sc[c] = acc_sc[c] * alpha + _dot(vt_aug, p)
            m_sc[c] = m_new

    n_full = i
    n_plain = jnp.maximum(n_full - 1, 0)
    scores(0, 0, True)

    def pair(p, carry):
        t = 2 * p
        scores(t + 1, 1, False)
        consume(t, 0)
        scores(t + 2, 0, False)
        consume(t + 1, 1)
        return carry

    lax.fori_loop(0, n_plain // 2, pair, 0)
    tc = (n_plain // 2) * 2

    @pl.when(n_full == 0)
    def _():
        consume(0, 0)

    @pl.when(jnp.logical_and(n_full > 0, n_plain % 2 == 0))
    def _():
        scores(tc + 1, 1, True)
        consume(tc, 0)
        consume(tc + 1, 1)

    @pl.when(jnp.logical_and(n_full > 0, n_plain % 2 == 1))
    def _():
        scores(tc + 1, 1, False)
        consume(tc, 0)
        scores(tc + 2, 0, True)
        consume(tc + 1, 1)
        consume(tc + 2, 0)

    a0, a1 = acc_sc[0], acc_sc[1]
    lam = _attn_lambda(lq_ref, lam_init)
    o_t = a0[:A_VDIM] / a0[A_VDIM:A_VDIM + 1] - lam * (a1[:A_VDIM] / a1[A_VDIM:A_VDIM + 1])
    o_ref[0] = _attn_norm(o_t.T, sub_ref[...], lam_init)


def _lane_cat(parts):
    return parts[0] if len(parts) == 1 else jnp.concatenate(parts, axis=1)


def _attn_prompt_call(lq, sub, qb, kb, vt, lam_init):
    b, t, _ = kb.shape
    tv = vt.shape[-1]
    tq = min(t, 512)
    tk = tq
    assert t % tq == 0 and (tq % CHUNK == 0 or tq == t) and tq % LANES == 0 and tk % tv == 0
    return pl.pallas_call(
        functools.partial(_attn_prompt_kernel, tq=tq, tk=tk, tv=tv, lam_init=lam_init),
        grid=(b, A_HEADS, t // tq),
        in_specs=[pl.BlockSpec(lq.shape, lambda bi, h, i: (0, 0)),
                  pl.BlockSpec(sub.shape, lambda bi, h, i: (0, 0)),
                  pl.BlockSpec((1, tq, 2 * LANES), lambda bi, h, i: (bi, i, h)),
                  pl.BlockSpec((1, t, LANES), lambda bi, h, i: (bi, 0, h)),
                  pl.BlockSpec((1, 1, t // tv, A_VDIM, tv), lambda bi, h, i: (bi, h, 0, 0, 0))],
        out_specs=pl.BlockSpec((1, tq, LANES), lambda bi, h, i: (bi, i, h)),
        out_shape=jax.ShapeDtypeStruct((b, t, A_WIDTH), BF16),
        scratch_shapes=[pltpu.VMEM((2, 1, tq), F32), pltpu.VMEM((2, A_VDIM + BF16_SUBLANES, tq), F32),
                        pltpu.VMEM((2, 2, tk, tq), F32), pltpu.VMEM((2, 2, 1, tq), F32)],
        compiler_params=_cparams(("arbitrary", "arbitrary", "arbitrary")),
        name="attn_prompt",
    )(lq, sub, qb, kb, vt)


def _attn_cached_kernel(lq_ref, sub_ref, q_ref, kpt_ref, vp_ref, kn_ref, vn_ref, o_ref, *, t, p_len, lam_init):
    mask_p = jnp.concatenate([_chunk_mask(p_len, 0, t, p_len)] * 2, axis=0)
    mask_n = jnp.concatenate([_chunk_mask(p_len, p_len, t, t)] * 2, axis=0)
    lam = _attn_lambda(lq_ref, lam_init)
    ones_p = jnp.ones((p_len, LANES), BF16)
    ones_n = jnp.ones((t, LANES), BF16)
    for h in range(A_HEADS):
        cols = slice(h * LANES, (h + 1) * LANES)
        kpt = kpt_ref[0, 0, cols, :].astype(BF16)
        kn = kn_ref[0, :, cols]
        vp = jnp.concatenate([vp_ref[0, 0, pl.ds(h, p_len, stride=A_HEADS), :].astype(BF16), ones_p], axis=1)
        vn = jnp.concatenate([vn_ref[0, 0, pl.ds(h, t, stride=A_HEADS), :].astype(BF16), ones_n], axis=1)
        q2 = jnp.concatenate([q_ref[0, :, 2 * h * LANES:(2 * h + 1) * LANES],
                              q_ref[0, :, (2 * h + 1) * LANES:(2 * h + 2) * LANES]], axis=0)
        s_p = jnp.where(mask_p, _dot(q2, kpt), NEG_BIG)
        s_n = jnp.where(mask_n, _dot_nt(q2, kn), NEG_BIG)
        m = jnp.maximum(jnp.max(s_p, axis=1, keepdims=True), jnp.max(s_n, axis=1, keepdims=True))
        acc = _dot(jnp.exp2(s_p - m).astype(BF16), vp) + _dot(jnp.exp2(s_n - m).astype(BF16), vn)
        ratio = acc[:, :LANES] / acc[:, LANES:]
        o = ratio[:t] - lam * ratio[t:]
        o_ref[0, :, cols] = _attn_norm(o, sub_ref[...], lam_init)


def _attn_cached_call(lq, sub, qb, k_new, v_new_rows, k_past_t, v_past_rows, layer, lam_init):
    b, t, _ = k_new.shape
    p_len = k_past_t.shape[-1]
    aw = 2 * A_HEADS * A_HEAD_DIM
    return pl.pallas_call(
        functools.partial(_attn_cached_kernel, t=t, p_len=p_len, lam_init=lam_init),
        grid=(b,),
        in_specs=[pl.BlockSpec(lq.shape, lambda bi: (0, 0)),
                  pl.BlockSpec(sub.shape, lambda bi: (0, 0)),
                  pl.BlockSpec((1, t, 2 * aw), lambda bi: (bi, 0, 0)),
                  pl.BlockSpec((1, 1, aw, p_len), lambda bi: (layer, bi, 0, 0)),
                  pl.BlockSpec((1, 1, p_len * A_HEADS, A_VDIM), lambda bi: (layer, bi, 0, 0)),
                  pl.BlockSpec((1, t, aw), lambda bi: (bi, 0, 0)),
                  pl.BlockSpec((1, 1, t * A_HEADS, A_VDIM), lambda bi: (layer, bi, 0, 0))],
        out_specs=pl.BlockSpec((1, t, A_WIDTH), lambda bi: (bi, 0, 0)),
        out_shape=jax.ShapeDtypeStruct((b, t, A_WIDTH), BF16),
        compiler_params=_cparams(("arbitrary",)),
        name="attn_cached",
    )(lq, sub, qb, k_past_t, v_past_rows, k_new, v_new_rows)


def _split3(x):
    hi = x.astype(BF16)
    r1 = x - hi.astype(F32)
    mid = r1.astype(BF16)
    lo = (r1 - mid.astype(F32)).astype(BF16)
    return hi, mid, lo


def _gla_kernel(q_ref, k_ref, v_ref, r_ref, la_ref, gn_ref, st0_ref, o_ref, sto_ref, st, *, tg, lc):
    t = pl.program_id(1)

    @pl.when(t == 0)
    def _():
        st[...] = st0_ref[0]

    rr = lax.broadcasted_iota(jnp.int32, (lc, lc), 0)
    cc = lax.broadcasted_iota(jnp.int32, (lc, lc), 1)
    tri = rr >= cc
    tri_b = jnp.where(tri, 1.0, 0.0).astype(BF16)
    lane_k = lax.broadcasted_iota(jnp.int32, (lc, G_QK), 1) // G_DK
    bd = (lax.broadcasted_iota(jnp.int32, (G_WIDTH, G_QK), 0) // G_DV
          == lax.broadcasted_iota(jnp.int32, (G_WIDTH, G_QK), 1) // G_DK)
    gn = gn_ref[...]

    per_chunk = []
    for ci in range(tg // lc):
        rows = slice(ci * lc, (ci + 1) * lc)
        hi, mid, lo = _split3(la_ref[0, rows, :])
        bc = _dot(tri_b, hi) + _dot(tri_b, mid) + _dot(tri_b, lo)
        b_mid = bc[lc // 2 - 1:lc // 2]
        b_last = bc[lc - 1:lc]
        q = q_ref[0, rows, :]
        k = k_ref[0, rows, :]
        vb = v_ref[0, rows, :]
        q_in = (q * jnp.exp(bc)).astype(BF16)
        q_mid = q * jnp.exp(bc - b_mid)
        k_mid = (k * jnp.exp(b_mid - bc)).astype(BF16)
        k_end = (k * jnp.exp(b_last - bc)).astype(BF16)
        att = []
        for h in range(G_HEADS):
            qh = jnp.where(lane_k == h, q_mid, 0.0).astype(BF16)
            att.append(jnp.where(tri, _dot_nt(qh, k_mid), 0.0).astype(BF16))
        incr = jnp.where(bd, _dot_tn(vb, k_end), 0.0)
        per_chunk.append((rows, q_in, att, vb, incr, jnp.exp(b_last)))

    intras = [[_dot(att[h], vb[:, h * G_DV:(h + 1) * G_DV]) for h in range(G_HEADS)]
              for _, _, att, vb, _, _ in per_chunk]

    s_cur = st[...]
    for (rows, q_in, _, _, incr, decay), intra in zip(per_chunk, intras):
        inter = _dot_nt(q_in, s_cur.astype(BF16))
        outs = []
        for h in range(G_HEADS):
            oh = inter[:, h * G_DV:(h + 1) * G_DV] + intra[h]
            ms = jnp.mean(oh * oh, axis=-1, keepdims=True)
            outs.append(oh * lax.rsqrt(ms + EPS) * gn)
        gr = r_ref[0, rows, :]
        o_ref[0, rows, :] = (jnp.concatenate(outs, axis=1) * (gr * _sigmoid(gr))).astype(BF16)
        s_cur = s_cur * decay + incr
    st[...] = s_cur
    sto_ref[0] = s_cur


def _gla_call(gq, gk, gv, gr, la, gn, st0):
    b, t, _ = gq.shape
    lc = min(CHUNK, t)
    tg = min(t, 512)
    assert t % tg == 0 and tg % lc == 0 and lc % 16 == 0
    row = lambda bi, ti: (bi, ti, 0)
    st_spec = pl.BlockSpec((1, G_WIDTH, G_QK), lambda bi, ti: (bi, 0, 0))
    return pl.pallas_call(
        functools.partial(_gla_kernel, tg=tg, lc=lc),
        grid=(b, t // tg),
        in_specs=[pl.BlockSpec((1, tg, G_QK), row), pl.BlockSpec((1, tg, G_QK), row),
                  pl.BlockSpec((1, tg, G_WIDTH), row), pl.BlockSpec((1, tg, G_WIDTH), row),
                  pl.BlockSpec((1, tg, G_QK), row), pl.BlockSpec(gn.shape, lambda bi, ti: (0, 0)), st_spec],
        out_specs=[pl.BlockSpec((1, tg, G_WIDTH), row), st_spec],
        out_shape=[jax.ShapeDtypeStruct((b, t, G_WIDTH), BF16), jax.ShapeDtypeStruct((b, G_WIDTH, G_QK), F32)],
        scratch_shapes=[pltpu.VMEM((G_WIDTH, G_QK), F32)],
        compiler_params=_cparams(("arbitrary", "arbitrary")),
        name="gla",
    )(gq, gk, gv, gr, la, gn, st0)


def _merge_kernel(x_ref, ao_ref, go_ref, lo_ref, nm_ref, wm_ref, bm_ref, wba_ref, wbg_ref, wbl_ref, wo_ref, o_ref):
    x = x_ref[...]
    xb = _rms(x, nm_ref[...]).astype(BF16)
    g = _sigmoid(_dot(xb, wm_ref[...]) + bm_ref[...])
    merged = (g[:, :D_MODEL] * _dot(ao_ref[...], wba_ref[...])
              + g[:, D_MODEL:2 * D_MODEL] * _dot(go_ref[...], wbg_ref[...])
              + g[:, 2 * D_MODEL:] * _dot(lo_ref[...], wbl_ref[...]))
    o_ref[...] = x + _dot(merged.astype(BF16), wo_ref[...])


def _merge_call(x2, ao, go, lo, w):
    n, d = x2.shape
    tm = min(n, 512)
    assert n % tm == 0
    row = lambda i: (i, 0)
    weights = (w['norm_mix'], w['w_merge'], w['b_merge'], w['w_branch_attn'], w['w_branch_gla'], w['w_branch_lru'],
               w['w_out'])
    wspecs = [pl.BlockSpec(a.shape, lambda i: (0, 0), pipeline_mode=pl.Buffered(1)) for a in weights]
    return pl.pallas_call(
        _merge_kernel,
        grid=(n // tm,),
        in_specs=[pl.BlockSpec((tm, d), row), pl.BlockSpec((tm, A_WIDTH), row), pl.BlockSpec((tm, G_WIDTH), row),
                  pl.BlockSpec((tm, LRU_WIDTH), row)] + wspecs,
        out_specs=pl.BlockSpec((tm, d), row),
        out_shape=jax.ShapeDtypeStruct((n, d), F32),
        compiler_params=_cparams(("arbitrary",)),
        name="merge",
    )(x2, ao, go, lo, *weights)


def _ffn_kernel(x_ref, nf_ref, wg_ref, cw_ref, cb_ref, wu_ref, wd_ref, nl_ref, buf0_ref, o_ref, bufo_ref, ubuf,
                *, seqs, tm, final):
    t = pl.program_id(1)
    x = x_ref[...].reshape(seqs * tm, D_MODEL)
    hb = _rms(x, nf_ref[...]).astype(BF16)

    @pl.when(t == 0)
    def _():
        ubuf[...] = buf0_ref[...]

    gu = _dot(hb, wg_ref[...])
    gc = _causal_conv(gu, ubuf[...], cw_ref[...], cb_ref[...])
    tail = _last_groups(gu, seqs)
    bufo_ref[...] = tail
    ubuf[...] = tail
    f = _gelu_tanh(gc) * _dot(hb, wu_ref[...])
    y = x + _dot(f.astype(BF16), wd_ref[...])
    y = _rms(y, nl_ref[...]) if final else y
    o_ref[...] = y.reshape(seqs, tm, D_MODEL)


def _ffn_call(x, w, norm_last, buf0, final):
    b, t, d = x.shape
    tm = min(t, 512)
    assert t % tm == 0
    seqs = max(1, min(b, 256 // tm)) if tm == t else 1
    assert b % seqs == 0
    row = lambda bi, ti: (bi, ti, 0)
    const2 = lambda bi, ti: (0, 0)
    buf_spec = pl.BlockSpec((seqs, CARRY_ROWS, D_FF), lambda bi, ti: (bi, 0, 0))
    weights = (w['norm_ffn'], w['w_ffn_gate'], w['ffn_conv_w'], w['ffn_conv_b'], w['w_ffn_up'], w['w_ffn_down'],
               norm_last)
    wspecs = [pl.BlockSpec(a.shape, const2, pipeline_mode=pl.Buffered(1)) for a in weights]
    return pl.pallas_call(
        functools.partial(_ffn_kernel, seqs=seqs, tm=tm, final=final),
        grid=(b // seqs, t // tm),
        in_specs=[pl.BlockSpec((seqs, tm, d), row)] + wspecs + [buf_spec],
        out_specs=[pl.BlockSpec((seqs, tm, d), row), buf_spec],
        out_shape=[jax.ShapeDtypeStruct((b, t, d), F32), jax.ShapeDtypeStruct((b, CARRY_ROWS, D_FF), F32)],
        scratch_shapes=[pltpu.VMEM((seqs, CARRY_ROWS, D_FF), F32)],
        compiler_params=_cparams(("arbitrary", "arbitrary")),
        name="ffn",
    )(x, *weights, buf0)


def _rope_tables(p_len, t):
    half = ROPE_DIM // 2
    pos = (p_len + jnp.arange(t, dtype=jnp.int32)).astype(F32)
    inv = ROPE_THETA ** (-jnp.arange(half, dtype=F32) / half)
    ang = pos[:, None] * inv[None, :]
    cos, sin = jnp.cos(ang), jnp.sin(ang)
    rest = A_HEAD_DIM - ROPE_DIM
    z = jnp.zeros((t, half), F32)
    c64 = jnp.concatenate([cos, cos, jnp.ones((t, rest), F32)], axis=1)
    sa64 = jnp.concatenate([-sin, z, jnp.zeros((t, rest), F32)], axis=1)
    sb64 = jnp.concatenate([z, sin, jnp.zeros((t, rest), F32)], axis=1)
    rep = LANES // A_HEAD_DIM
    return tuple(jnp.tile(a, (1, rep)) for a in (c64, sa64, sb64))


def _block_diag(wb):
    n, bi, bo = wb.shape
    eye = jnp.eye(n, dtype=wb.dtype)
    return (eye[:, None, :, None] * wb[:, :, None, :]).reshape(n * bi, n * bo)


def _prep_layer(l, p):
    aw = 2 * A_HEADS * A_HEAD_DIM
    o_gla = 2 * aw + A_WIDTH
    o_ga = o_gla + 2 * G_QK + 2 * G_WIDTH
    o_lru = o_ga + G_GATE_RANK
    w_in = p['w_in'][l]
    row = lambda a: a[l].reshape(1, -1)
    return {
        'norm_mix': row(p['norm_mix']),
        'wqkv': w_in[:, :o_gla].astype(BF16),
        'wgla': w_in[:, o_gla:o_ga].astype(BF16),
        'wga': w_in[:, o_ga:o_lru].astype(BF16),
        'wlru': w_in[:, o_lru:].astype(BF16),
        'wg2': p['w_gla_gate2'][l].astype(BF16),
        'bg2': row(p['b_gla_gate']),
        'lambda_qk': p['lambda_qk'][l],
        'attn_subln': row(p['attn_subln']),
        'gla_norm': row(p['gla_norm']),
        'lru_conv_w': p['lru_conv_w'][l],
        'lru_conv_b': row(p['lru_conv_b']),
        'wa_bd': _block_diag(p['lru_wa'][l]).astype(BF16),
        'lru_ba': row(p['lru_ba']),
        'wx_bd': _block_diag(p['lru_wx'][l]).astype(BF16),
        'lru_bx': row(p['lru_bx']),
        'lru_lambda': row(p['lru_lambda']),
        'w_branch_attn': p['w_branch_attn'][l].astype(BF16),
        'w_branch_gla': p['w_branch_gla'][l].astype(BF16),
        'w_branch_lru': p['w_branch_lru'][l].astype(BF16),
        'w_merge': p['w_merge'][l].astype(BF16),
        'b_merge': row(p['b_merge']),
        'w_out': p['w_out'][l].astype(BF16),
        'norm_ffn': row(p['norm_ffn']),
        'w_ffn_gate': p['w_ffn_gate'][l].astype(BF16),
        'ffn_conv_w': p['ffn_conv_w'][l],
        'ffn_conv_b': row(p['ffn_conv_b']),
        'w_ffn_up': p['w_ffn_up'][l].astype(BF16),
        'w_ffn_down': p['w_ffn_down'][l].astype(BF16),
    }


def _pad_carry(buf):
    return jnp.pad(buf, ((0, 0), (CARRY_ROWS - buf.shape[1], 0), (0, 0)))


def _state_to_kernel(s):
    b = s.shape[0]
    eye = jnp.eye(G_HEADS, dtype=s.dtype)
    st = jnp.swapaxes(s, 2, 3)
    return (st[:, :, :, None, :] * eye[None, :, None, :, None]).reshape(b, G_WIDTH, G_QK)


def _state_from_kernel(st):
    b = st.shape[0]
    s5 = st.reshape(b, G_HEADS, G_DV, G_HEADS, G_DK)
    diag = jnp.stack([s5[:, h, :, h, :] for h in range(G_HEADS)], axis=1)
    return jnp.swapaxes(diag, 2, 3)


def _trunk(x, caches, layers, norm_final):
    cache_k, cache_v, st_gla, st_lconv, st_lh, st_fconv = caches
    b, t, d = x.shape
    p_len = 0 if cache_k is None else cache_k.shape[2]
    tabs = _rope_tables(p_len, t)
    if cache_k is not None:
        depth = cache_k.shape[0]
        cache_kt = jnp.transpose(cache_k, (0, 1, 3, 4, 5, 2)).reshape(depth, b, -1, p_len)
        cache_vr = cache_v.reshape(depth, b, p_len * A_HEADS, A_VDIM)
    outs = [[], [], [], [], [], []]
    stacks = None
    for l, w in enumerate(layers):
        lam_init = 0.8 - 0.6 * math.exp(-0.3 * l)
        (kt_all, vr_all, qb, kb, vt, gq, gk, gv, gr, la, lo, lbuf, hl) = _proj_call(
            x, tabs, w, _pad_carry(st_lconv[l]), st_lh[l].reshape(b, 1, LRU_WIDTH), l, len(layers), stacks)
        stacks = (kt_all, vr_all)
        if cache_k is None:
            ao = _attn_prompt_call(w['lambda_qk'], w['attn_subln'], qb, kb, vt, lam_init)
        else:
            ao = _attn_cached_call(w['lambda_qk'], w['attn_subln'], qb, kb, vr_all, cache_kt, cache_vr, l, lam_init)
        go, st_new = _gla_call(gq, gk, gv, gr, la, w['gla_norm'], _state_to_kernel(st_gla[l]))
        x1 = _merge_call(x.reshape(b * t, d), ao.reshape(b * t, -1), go.reshape(b * t, -1), lo.reshape(b * t, -1), w)
        x, fbuf = _ffn_call(x1.reshape(b, t, d), w, norm_final.reshape(1, -1), _pad_carry(st_fconv[l]),
                            final=(l == len(layers) - 1))
        outs[2].append(_state_from_kernel(st_new))
        outs[3].append(lbuf[:, CARRY_ROWS - (LRU_CONV - 1):])
        outs[4].append(hl.reshape(b, LRU_WIDTH))
        outs[5].append(fbuf[:, CARRY_ROWS - (FFN_CONV - 1):])
    kt_all, vr_all = stacks
    depth = len(layers)
    new_k = jnp.transpose(kt_all.reshape(depth, b, A_HEADS, 2, A_HEAD_DIM, t), (0, 1, 5, 2, 3, 4))
    new_v = vr_all.reshape(depth, b, t, A_HEADS, A_VDIM)
    return x, [new_k, new_v] + [jnp.stack(o) for o in outs[2:]]


@jax.jit
def _forward(x_prompt, x_sample, cache_attn_k, cache_attn_v, state_gla, state_lru_conv, state_lru_h, state_ffn_conv,
             params, norm_final):
    layers = [_prep_layer(l, params) for l in range(DEPTH)]
    bp = x_prompt.shape[0]
    zeros = (None, None,
             jnp.zeros((DEPTH, bp, G_HEADS, G_DK, G_DV), F32),
             jnp.zeros((DEPTH, bp, LRU_CONV - 1, LRU_WIDTH), F32),
             jnp.zeros((DEPTH, bp, LRU_WIDTH), F32),
             jnp.zeros((DEPTH, bp, FFN_CONV - 1, D_FF), F32))
    y_p, new_p = _trunk(x_prompt, zeros, layers, norm_final)
    y_s, new_s = _trunk(x_sample, (cache_attn_k, cache_attn_v, state_gla, state_lru_conv, state_lru_h,
                                   state_ffn_conv), layers, norm_final)
    return (y_p, y_s, *new_p, *new_s)


def kernel(x_prompt, x_sample, cache_attn_k, cache_attn_v, state_gla, state_lru_conv, state_lru_h, state_ffn_conv, norm_mix, w_in, lambda_qk, attn_subln, w_gla_gate2, b_gla_gate, gla_norm, lru_conv_w, lru_conv_b, lru_wa, lru_ba, lru_wx, lru_bx, lru_lambda, w_branch_attn, w_branch_gla, w_branch_lru, w_merge, b_merge, w_out, norm_ffn, w_ffn_gate, ffn_conv_w, ffn_conv_b, w_ffn_up, w_ffn_down, norm_final):
    params = dict(norm_mix=norm_mix, w_in=w_in, lambda_qk=lambda_qk, attn_subln=attn_subln, w_gla_gate2=w_gla_gate2,
                  b_gla_gate=b_gla_gate, gla_norm=gla_norm, lru_conv_w=lru_conv_w, lru_conv_b=lru_conv_b,
                  lru_wa=lru_wa, lru_ba=lru_ba, lru_wx=lru_wx, lru_bx=lru_bx, lru_lambda=lru_lambda,
                  w_branch_attn=w_branch_attn, w_branch_gla=w_branch_gla, w_branch_lru=w_branch_lru,
                  w_merge=w_merge, b_merge=b_merge, w_out=w_out, norm_ffn=norm_ffn, w_ffn_gate=w_ffn_gate,
                  ffn_conv_w=ffn_conv_w, ffn_conv_b=ffn_conv_b, w_ffn_up=w_ffn_up, w_ffn_down=w_ffn_down)
    return _forward(x_prompt, x_sample, cache_attn_k, cache_attn_v, state_gla, state_lru_conv, state_lru_h,
                    state_ffn_conv, params, norm_final)
```

```python
import functools
import math

import jax
import jax.numpy as jnp
from jax import lax
from jax.experimental import pallas as pl
from jax.experimental.pallas import tpu as pltpu

F32 = jnp.float32
BF16 = jnp.bfloat16

D_MODEL = 1024
DEPTH = 4
CHUNK = 64
EPS = 1e-6
A_HEADS = 4
A_HEAD_DIM = 64
A_VDIM = 128
A_WIDTH = A_HEADS * A_VDIM
ROPE_DIM = 16
ROPE_THETA = 500000.0
G_HEADS = 4
G_DK = 64
G_DV = 128
G_QK = G_HEADS * G_DK
G_WIDTH = G_HEADS * G_DV
G_GATE_RANK = 16
G_GATE_TAU = 16.0
LRU_WIDTH = 512
LRU_BLOCKS = 8
LRU_CONV = 4
LRU_C = 8.0
D_FF = 2816
FFN_CONV = 3

LANES = 128
SUBLANES = 8
BF16_SUBLANES = 16
LOG2E = math.log2(math.e)
CARRY_ROWS = 8
NEG_BIG = -1e30
VMEM_LIMIT = 56 * 1024 * 1024


def _cparams(sem):
    return pltpu.CompilerParams(dimension_semantics=sem, vmem_limit_bytes=VMEM_LIMIT)


def _rms(x, g):
    return x * lax.rsqrt(jnp.mean(x * x, axis=-1, keepdims=True) + EPS) * g


def _sigmoid(x):
    return 0.5 * jnp.tanh(0.5 * x) + 0.5


def _sqrt_bounded(x):
    return jnp.where(x == 0.0, 0.0, x * lax.rsqrt(x))


def _softplus(x):
    return jnp.maximum(x, 0.0) + jnp.log1p(jnp.exp(-jnp.abs(x)))


def _gelu_tanh(x):
    return x * (0.5 * (1.0 + jnp.tanh(math.sqrt(2.0 / math.pi) * (x + 0.044715 * (x * x * x)))))


def _dot(a, b):
    return jnp.dot(a, b, preferred_element_type=F32)


def _dot_nt(a, b):
    return lax.dot_general(a, b, (((1,), (1,)), ((), ())), preferred_element_type=F32)


def _dot_tn(a, b):
    return lax.dot_general(a, b, (((0,), (0,)), ((), ())), preferred_element_type=F32)


def _lane_tile(x, n):
    return x if n == 1 else jnp.concatenate([x] * n, axis=1)


def _rope(x, c, sa, sb):
    segs = []
    for g in range(x.shape[1] // LANES):
        seg = x[:, g * LANES:(g + 1) * LANES]
        seg_up = pltpu.roll(seg, LANES - ROPE_DIM // 2, 1)
        seg_dn = pltpu.roll(seg, ROPE_DIM // 2, 1)
        segs.append(seg * c + seg_up * sa + seg_dn * sb)
    return jnp.concatenate(segs, axis=1)


def _causal_conv(x, prev, w, bias):
    seqs = prev.shape[0]
    rows, width = x.shape
    taps = w.shape[0]
    gps = rows // seqs // SUBLANES
    x3 = x.reshape(seqs * gps, SUBLANES, width)
    ext = []
    for s in range(seqs):
        ext += [prev[s:s + 1], x3[s * gps:(s + 1) * gps]]
    ext = jnp.concatenate(ext, axis=0)

    def pick(r, first):
        parts = [r[s * (gps + 1) + first:s * (gps + 1) + first + gps] for s in range(seqs)]
        return parts[0] if seqs == 1 else jnp.concatenate(parts, axis=0)

    sub = lax.broadcasted_iota(jnp.int32, x3.shape, 1)
    y = bias + x3 * w[taps - 1:taps]
    for d in range(1, taps):
        r = pltpu.roll(ext, d, 1)
        y = y + jnp.where(sub >= d, pick(r, 1), pick(r, 0)) * w[taps - 1 - d:taps - d]
    return y.reshape(rows, width)


def _last_groups(x, seqs):
    per = x.shape[0] // seqs
    tails = [x[(s + 1) * per - SUBLANES:(s + 1) * per] for s in range(seqs)]
    return jnp.stack(tails, axis=0)


def _lin_scan(a, u, h_in, h_ref):
    tm, width = a.shape
    groups = tm // SUBLANES
    a = a.reshape(groups, SUBLANES, width)
    u = u.reshape(groups, SUBLANES, width)
    sub = lax.broadcasted_iota(jnp.int32, a.shape, 1)
    d = 1
    while d < SUBLANES:
        a_s = pltpu.roll(a, d, 1)
        u_s = pltpu.roll(u, d, 1)
        valid = sub >= d
        u = jnp.where(valid, a * u_s + u, u)
        a = jnp.where(valid, a * a_s, a)
        d *= 2
    carry = h_in
    for g in range(groups):
        h_ref[g * SUBLANES:(g + 1) * SUBLANES] = a[g] * carry + u[g]
        carry = h_ref[(g + 1) * SUBLANES - 1:(g + 1) * SUBLANES]


def _proj_kernel(x_ref, cos_ref, sa_ref, sb_ref, nm_ref, wqkv_ref, wgla_ref, wga_ref, wg2_ref, bg2_ref, wlru_ref,
                 cw_ref, cb_ref, wa_ref, ba_ref, wx_ref, bx_ref, lam_ref, buf0_ref, h0_ref,
                 kt_ref, vr_ref, qb_ref, kb_ref, vt_ref, gq_ref, gk_ref, gv_ref, gr_ref, la_ref, lo_ref,
                 bufo_ref, hl_ref, xbuf, hc, hbuf, *, tm):
    t = pl.program_id(1)
    xn = _rms(x_ref[0], nm_ref[...])
    xb = xn.astype(BF16)

    lx = _dot(xb, wlru_ref[:, :LRU_WIDTH])
    lg = _dot(xb, wlru_ref[:, LRU_WIDTH:])

    @pl.when(t == 0)
    def _():
        xbuf[...] = buf0_ref[...]
        hc[...] = h0_ref[0]

    xc = _causal_conv(lx, xbuf[...], cw_ref[...], cb_ref[...])
    tail = _last_groups(lx, 1)
    bufo_ref[...] = tail
    xbuf[...] = tail

    xcb = xc.astype(BF16)
    r_pre = _dot(xcb, wa_ref[...]) + ba_ref[...]
    i_pre = _dot(xcb, wx_ref[...]) + bx_ref[...]

    c, sa, sb = cos_ref[...], sa_ref[...], sb_ref[...]
    aw = 2 * A_HEADS * A_HEAD_DIM
    q = _rope(_dot(xb, wqkv_ref[:, :aw]), c, sa, sb) * (A_HEAD_DIM ** -0.5 * LOG2E)
    k = _rope(_dot(xb, wqkv_ref[:, aw:2 * aw]), c, sa, sb)
    v = _dot(xb, wqkv_ref[:, 2 * aw:])
    kt_ref[0, 0] = k.T
    for h in range(A_HEADS):
        vr_ref[0, 0, pl.ds(h, tm, stride=A_HEADS), :] = v[:, h * A_VDIM:(h + 1) * A_VDIM]
    kb_ref[0] = k.astype(BF16)
    vt_ref[0, :, 0] = v.T.reshape(A_HEADS, A_VDIM, tm).astype(BF16)
    lane = lax.broadcasted_iota(jnp.int32, (tm, LANES), 1)
    for h in range(A_HEADS):
        seg = q[:, h * LANES:(h + 1) * LANES]
        qb_ref[0, :, 2 * h * LANES:(2 * h + 1) * LANES] = jnp.where(lane < A_HEAD_DIM, seg, 0.0).astype(BF16)
        qb_ref[0, :, (2 * h + 1) * LANES:(2 * h + 2) * LANES] = jnp.where(lane >= A_HEAD_DIM, seg, 0.0).astype(BF16)

    gq_ref[0] = _dot(xb, wgla_ref[:, :G_QK]) * (G_DK ** -0.5)
    gk_ref[0] = _dot(xb, wgla_ref[:, G_QK:2 * G_QK])
    gv_ref[0] = _dot(xb, wgla_ref[:, 2 * G_QK:2 * G_QK + G_WIDTH]).astype(BF16)
    gr_ref[0] = _dot(xb, wgla_ref[:, 2 * G_QK + G_WIDTH:])
    ga = _dot(xb, wga_ref[...])
    gate = _dot(ga.astype(BF16), wg2_ref[...]) + bg2_ref[...]
    la_ref[0] = -_softplus(-gate) * (1.0 / G_GATE_TAU)

    log_a = (-LRU_C) * _sigmoid(r_pre) * _softplus(-lam_ref[...])
    a = jnp.exp(log_a)
    u = _sqrt_bounded(-jnp.tanh(log_a) * (a * a + 1.0)) * (_sigmoid(i_pre) * xc)
    _lin_scan(a, u, hc[...], hbuf)
    h_last = hbuf[tm - 1:tm]
    hc[...] = h_last
    hl_ref[0] = h_last
    lo_ref[0] = (hbuf[...] * _gelu_tanh(lg)).astype(BF16)


def _proj_kernel_stacked(kst_ref, vst_ref, *refs, tm):
    del kst_ref, vst_ref
    _proj_kernel(*refs, tm=tm)


def _proj_call(x, tabs, w, buf0, h0, layer, depth, stacks):
    b, t, d = x.shape
    tm = min(t, 512)
    assert t % tm == 0 and tm % 8 == 0
    nt = t // tm
    row = lambda bi, ti: (bi, ti, 0)
    const2 = lambda bi, ti: (0, 0)
    tab_spec = pl.BlockSpec((tm, LANES), lambda bi, ti: (ti, 0))

    def wspec(a):
        return pl.BlockSpec(a.shape, const2, pipeline_mode=pl.Buffered(1))

    weights = (w['norm_mix'], w['wqkv'], w['wgla'], w['wga'], w['wg2'], w['bg2'], w['wlru'], w['lru_conv_w'],
               w['lru_conv_b'], w['wa_bd'], w['lru_ba'], w['wx_bd'], w['lru_bx'], w['lru_lambda'])
    in_specs = ([pl.BlockSpec((1, tm, d), row), tab_spec, tab_spec, tab_spec] + [wspec(a) for a in weights]
                + [pl.BlockSpec((1, CARRY_ROWS, LRU_WIDTH), lambda bi, ti: (bi, 0, 0)),
                   pl.BlockSpec((1, 1, LRU_WIDTH), lambda bi, ti: (bi, 0, 0))])
    aw = 2 * A_HEADS * A_HEAD_DIM

    def out(wd, dt):
        return jax.ShapeDtypeStruct((b, t, wd), dt), pl.BlockSpec((1, tm, wd), row)

    vt_out = (jax.ShapeDtypeStruct((b, A_HEADS, nt, A_VDIM, tm), BF16),
              pl.BlockSpec((1, A_HEADS, 1, A_VDIM, tm), lambda bi, ti: (bi, 0, ti, 0, 0)))
    kt_out = (jax.ShapeDtypeStruct((depth, b, aw, t), F32),
              pl.BlockSpec((1, 1, aw, tm), lambda bi, ti: (layer, bi, 0, ti)))
    vr_out = (jax.ShapeDtypeStruct((depth, b, t * A_HEADS, A_VDIM), F32),
              pl.BlockSpec((1, 1, tm * A_HEADS, A_VDIM), lambda bi, ti: (layer, bi, ti, 0)))
    outs = [kt_out, vr_out, out(2 * aw, BF16), out(aw, BF16), vt_out,
            out(G_QK, F32), out(G_QK, F32), out(G_WIDTH, BF16), out(G_WIDTH, F32), out(G_QK, F32),
            out(LRU_WIDTH, BF16),
            (jax.ShapeDtypeStruct((b, CARRY_ROWS, LRU_WIDTH), F32),
             pl.BlockSpec((1, CARRY_ROWS, LRU_WIDTH), lambda bi, ti: (bi, 0, 0))),
            (jax.ShapeDtypeStruct((b, 1, LRU_WIDTH), F32), pl.BlockSpec((1, 1, LRU_WIDTH), lambda bi, ti: (bi, 0, 0)))]
    args = (x, *tabs, *weights, buf0, h0)
    body, aliases = _proj_kernel, {}
    if stacks is not None:
        args = (*stacks, *args)
        in_specs = [pl.BlockSpec(memory_space=pl.ANY)] * 2 + in_specs
        body, aliases = _proj_kernel_stacked, {0: 0, 1: 1}
    return pl.pallas_call(
        functools.partial(body, tm=tm),
        grid=(b, nt),
        in_specs=in_specs,
        out_specs=[o[1] for o in outs],
        out_shape=[o[0] for o in outs],
        input_output_aliases=aliases,
        scratch_shapes=[pltpu.VMEM((1, CARRY_ROWS, LRU_WIDTH), F32), pltpu.VMEM((1, LRU_WIDTH), F32),
                        pltpu.VMEM((tm, LRU_WIDTH), F32)],
        compiler_params=_cparams(("arbitrary", "arbitrary")),
        name="proj",
    )(*args)


def _attn_lambda(lq_ref, lam_init):
    lq = lq_ref[...]
    s1 = jnp.sum(lq[0:1] * lq[1:2], axis=1, keepdims=True)
    s2 = jnp.sum(lq[2:3] * lq[3:4], axis=1, keepdims=True)
    return jnp.exp(s1) - jnp.exp(s2) + lam_init


def _attn_norm(o, sub, lam_init):
    ms = jnp.mean(o * o, axis=-1, keepdims=True)
    return (o * lax.rsqrt(ms + EPS) * sub * (1.0 - lam_init)).astype(BF16)


def _chunk_mask(q0, k0, tq, tk, keys_on_rows=False):
    shape = (tk, tq) if keys_on_rows else (tq, tk)
    qpos = q0 + lax.broadcasted_iota(jnp.int32, shape, 1 if keys_on_rows else 0)
    kpos = k0 + lax.broadcasted_iota(jnp.int32, shape, 0 if keys_on_rows else 1)
    return (kpos // CHUNK) <= (qpos // CHUNK)


def _attn_prompt_kernel(ti_ref, tj_ref, lq_ref, sub_ref, q_ref, k_ref, vt_ref, o_ref, m_sc, acc_sc, s_sc, mt_sc,
                        bias_sc, *, tq, tk, tv, n_tiles, lam_init):
    key_chunk = lax.broadcasted_iota(jnp.int32, (tk, tq), 0) // CHUNK
    qry_chunk = lax.broadcasted_iota(jnp.int32, (tk, tq), 1) // CHUNK
    bias_sc[0] = jnp.zeros((tk, tq), F32)
    bias_sc[1] = jnp.where(key_chunk <= qry_chunk, 0.0, NEG_BIG)
    m_sc[...] = jnp.full(m_sc.shape, NEG_BIG, F32)
    acc_sc[...] = jnp.zeros(acc_sc.shape, F32)
    ones = jnp.ones((BF16_SUBLANES, tk), BF16)
    lam = _attn_lambda(lq_ref, lam_init)
    sub = sub_ref[...]

    def scores(n, slot):
        n = jnp.minimum(n, n_tiles - 1)
        i, j = ti_ref[n], tj_ref[n]
        k = k_ref[0, pl.ds(pl.multiple_of(j * tk, tk), tk), :]
        q = q_ref[0, pl.ds(pl.multiple_of(i * tq, tq), tq), :]
        bias = bias_sc[jnp.where(i == j, 1, 0)]
        for c in range(2):
            s = _dot_nt(k, q[:, c * LANES:(c + 1) * LANES]) + bias
            s_sc[slot, c] = s
            mt_sc[slot, c] = jnp.max(s, axis=0, keepdims=True)

    def consume(n, slot):
        i, j = ti_ref[n], tj_ref[n]
        par = i % 2
        vt = [vt_ref[0, 0, j * (tk // tv) + u] for u in range(tk // tv)]
        vt_aug = jnp.concatenate([_lane_cat(vt), ones], axis=0)
        for c in range(2):
            m_prev = jnp.where(j == 0, NEG_BIG, m_sc[par, c])
            m_new = jnp.maximum(m_prev, mt_sc[slot, c])
            alpha = jnp.exp2(m_prev - m_new)
            p = jnp.exp2(s_sc[slot, c] - m_new).astype(BF16)
            acc_sc[par, c] = acc_sc[par, c] * alpha + _dot(vt_aug, p)
            m_sc[par, c] = m_new

    def emit(n):
        i, j = ti_ref[n], tj_ref[n]

        @pl.when(i == j)
        def _():
            a0, a1 = acc_sc[i % 2, 0], acc_sc[i % 2, 1]
            o_t = a0[:A_VDIM] / a0[A_VDIM:A_VDIM + 1] - lam * (a1[:A_VDIM] / a1[A_VDIM:A_VDIM + 1])
            o_ref[0, pl.ds(pl.multiple_of(i * tq, tq), tq), :] = _attn_norm(o_t.T, sub, lam_init)

    scores(0, 0)

    def pair(p, carry):
        n = 2 * p
        scores(n + 1, 1)
        consume(n, 0)
        scores(n + 2, 0)
        consume(n + 1, 1)
        emit(n)
        emit(n + 1)
        return carry

    lax.fori_loop(0, n_tiles // 2, pair, 0)
    if n_tiles % 2:
        consume(n_tiles - 1, 0)
        emit(n_tiles - 1)


def _lane_cat(parts):
    return parts[0] if len(parts) == 1 else jnp.concatenate(parts, axis=1)


def _attn_prompt_call(lq, sub, qb, kb, vt, lam_init):
    b, t, _ = kb.shape
    tv = vt.shape[-1]
    tq = min(t, 512)
    tk = tq
    assert t % tq == 0 and (tq % CHUNK == 0 or tq == t) and tq % LANES == 0 and tk % tv == 0
    nq = t // tq
    pairs = [(i, j) for i in range(nq) for j in range(i + 1)]
    ti = jnp.asarray([p[0] for p in pairs], jnp.int32)
    tj = jnp.asarray([p[1] for p in pairs], jnp.int32)
    grid_spec = pltpu.PrefetchScalarGridSpec(
        num_scalar_prefetch=2,
        grid=(b, A_HEADS),
        in_specs=[pl.BlockSpec(lq.shape, lambda bi, h, *_: (0, 0)),
                  pl.BlockSpec(sub.shape, lambda bi, h, *_: (0, 0)),
                  pl.BlockSpec((1, t, 2 * LANES), lambda bi, h, *_: (bi, 0, h)),
                  pl.BlockSpec((1, t, LANES), lambda bi, h, *_: (bi, 0, h)),
                  pl.BlockSpec((1, 1, t // tv, A_VDIM, tv), lambda bi, h, *_: (bi, h, 0, 0, 0))],
        out_specs=pl.BlockSpec((1, t, LANES), lambda bi, h, *_: (bi, 0, h)),
        scratch_shapes=[pltpu.VMEM((2, 2, 1, tq), F32), pltpu.VMEM((2, 2, A_VDIM + BF16_SUBLANES, tq), F32),
                        pltpu.VMEM((2, 2, tk, tq), F32), pltpu.VMEM((2, 2, 1, tq), F32),
                        pltpu.VMEM((2, tk, tq), F32)])
    return pl.pallas_call(
        functools.partial(_attn_prompt_kernel, tq=tq, tk=tk, tv=tv, n_tiles=len(pairs), lam_init=lam_init),
        grid_spec=grid_spec,
        out_shape=jax.ShapeDtypeStruct((b, t, A_WIDTH), BF16),
        compiler_params=_cparams(("arbitrary", "arbitrary")),
        name="attn_prompt",
    )(ti, tj, lq, sub, qb, kb, vt)


def _attn_cached_kernel(lq_ref, sub_ref, q_ref, kpt_ref, vp_ref, kn_ref, vn_ref, o_ref, *, t, p_len, lam_init):
    mask_p = jnp.concatenate([_chunk_mask(p_len, 0, t, p_len)] * 2, axis=0)
    mask_n = jnp.concatenate([_chunk_mask(p_len, p_len, t, t)] * 2, axis=0)
    lam = _attn_lambda(lq_ref, lam_init)
    ones_p = jnp.ones((p_len, LANES), BF16)
    ones_n = jnp.ones((t, LANES), BF16)
    for h in range(A_HEADS):
        cols = slice(h * LANES, (h + 1) * LANES)
        kpt = kpt_ref[0, 0, cols, :].astype(BF16)
        kn = kn_ref[0, :, cols]
        vp = jnp.concatenate([vp_ref[0, 0, pl.ds(h, p_len, stride=A_HEADS), :].astype(BF16), ones_p], axis=1)
        vn = jnp.concatenate([vn_ref[0, 0, pl.ds(h, t, stride=A_HEADS), :].astype(BF16), ones_n], axis=1)
        q2 = jnp.concatenate([q_ref[0, :, 2 * h * LANES:(2 * h + 1) * LANES],
                              q_ref[0, :, (2 * h + 1) * LANES:(2 * h + 2) * LANES]], axis=0)
        s_p = jnp.where(mask_p, _dot(q2, kpt), NEG_BIG)
        s_n = jnp.where(mask_n, _dot_nt(q2, kn), NEG_BIG)
        m = jnp.maximum(jnp.max(s_p, axis=1, keepdims=True), jnp.max(s_n, axis=1, keepdims=True))
        acc = _dot(jnp.exp2(s_p - m).astype(BF16), vp) + _dot(jnp.exp2(s_n - m).astype(BF16), vn)
        ratio = acc[:, :LANES] / acc[:, LANES:]
        o = ratio[:t] - lam * ratio[t:]
        o_ref[0, :, cols] = _attn_norm(o, sub_ref[...], lam_init)


def _attn_cached_call(lq, sub, qb, k_new, v_new_rows, k_past_t, v_past_rows, layer, lam_init):
    b, t, _ = k_new.shape
    p_len = k_past_t.shape[-1]
    aw = 2 * A_HEADS * A_HEAD_DIM
    return pl.pallas_call(
        functools.partial(_attn_cached_kernel, t=t, p_len=p_len, lam_init=lam_init),
        grid=(b,),
        in_specs=[pl.BlockSpec(lq.shape, lambda bi: (0, 0)),
                  pl.BlockSpec(sub.shape, lambda bi: (0, 0)),
                  pl.BlockSpec((1, t, 2 * aw), lambda bi: (bi, 0, 0)),
                  pl.BlockSpec((1, 1, aw, p_len), lambda bi: (layer, bi, 0, 0)),
                  pl.BlockSpec((1, 1, p_len * A_HEADS, A_VDIM), lambda bi: (layer, bi, 0, 0)),
                  pl.BlockSpec((1, t, aw), lambda bi: (bi, 0, 0)),
                  pl.BlockSpec((1, 1, t * A_HEADS, A_VDIM), lambda bi: (layer, bi, 0, 0))],
        out_specs=pl.BlockSpec((1, t, A_WIDTH), lambda bi: (bi, 0, 0)),
        out_shape=jax.ShapeDtypeStruct((b, t, A_WIDTH), BF16),
        compiler_params=_cparams(("arbitrary",)),
        name="attn_cached",
    )(lq, sub, qb, k_past_t, v_past_rows, k_new, v_new_rows)


def _split3(x):
    hi = x.astype(BF16)
    r1 = x - hi.astype(F32)
    mid = r1.astype(BF16)
    lo = (r1 - mid.astype(F32)).astype(BF16)
    return hi, mid, lo


def _gla_kernel(q_ref, k_ref, v_ref, r_ref, la_ref, gn_ref, st0_ref, o_ref, sto_ref, st, *, tg, lc):
    t = pl.program_id(1)

    @pl.when(t == 0)
    def _():
        st[...] = st0_ref[0]

    rr = lax.broadcasted_iota(jnp.int32, (lc, lc), 0)
    cc = lax.broadcasted_iota(jnp.int32, (lc, lc), 1)
    tri = rr >= cc
    tri_b = jnp.where(tri, 1.0, 0.0).astype(BF16)
    lane_k = lax.broadcasted_iota(jnp.int32, (lc, G_QK), 1) // G_DK
    bd = (lax.broadcasted_iota(jnp.int32, (G_WIDTH, G_QK), 0) // G_DV
          == lax.broadcasted_iota(jnp.int32, (G_WIDTH, G_QK), 1) // G_DK)
    gn = gn_ref[...]

    per_chunk = []
    for ci in range(tg // lc):
        rows = slice(ci * lc, (ci + 1) * lc)
        hi, mid, lo = _split3(la_ref[0, rows, :])
        bc = _dot(tri_b, hi) + _dot(tri_b, mid) + _dot(tri_b, lo)
        b_mid = bc[lc // 2 - 1:lc // 2]
        b_last = bc[lc - 1:lc]
        q = q_ref[0, rows, :]
        k = k_ref[0, rows, :]
        vb = v_ref[0, rows, :]
        q_in = (q * jnp.exp(bc)).astype(BF16)
        q_mid = q * jnp.exp(bc - b_mid)
        k_mid = (k * jnp.exp(b_mid - bc)).astype(BF16)
        k_end = (k * jnp.exp(b_last - bc)).astype(BF16)
        att = []
        for h in range(G_HEADS):
            qh = jnp.where(lane_k == h, q_mid, 0.0).astype(BF16)
            att.append(jnp.where(tri, _dot_nt(qh, k_mid), 0.0).astype(BF16))
        incr = jnp.where(bd, _dot_tn(vb, k_end), 0.0)
        per_chunk.append((rows, q_in, att, vb, incr, jnp.exp(b_last)))

    intras = [[_dot(att[h], vb[:, h * G_DV:(h + 1) * G_DV]) for h in range(G_HEADS)]
              for _, _, att, vb, _, _ in per_chunk]

    s_cur = st[...]
    for (rows, q_in, _, _, incr, decay), intra in zip(per_chunk, intras):
        inter = _dot_nt(q_in, s_cur.astype(BF16))
        outs = []
        for h in range(G_HEADS):
            oh = inter[:, h * G_DV:(h + 1) * G_DV] + intra[h]
            ms = jnp.mean(oh * oh, axis=-1, keepdims=True)
            outs.append(oh * lax.rsqrt(ms + EPS) * gn)
        gr = r_ref[0, rows, :]
        o_ref[0, rows, :] = (jnp.concatenate(outs, axis=1) * (gr * _sigmoid(gr))).astype(BF16)
        s_cur = s_cur * decay + incr
    st[...] = s_cur
    sto_ref[0] = s_cur


def _gla_call(gq, gk, gv, gr, la, gn, st0):
    b, t, _ = gq.shape
    lc = min(CHUNK, t)
    tg = min(t, 512)
    assert t % tg == 0 and tg % lc == 0 and lc % 16 == 0
    row = lambda bi, ti: (bi, ti, 0)
    st_spec = pl.BlockSpec((1, G_WIDTH, G_QK), lambda bi, ti: (bi, 0, 0))
    return pl.pallas_call(
        functools.partial(_gla_kernel, tg=tg, lc=lc),
        grid=(b, t // tg),
        in_specs=[pl.BlockSpec((1, tg, G_QK), row), pl.BlockSpec((1, tg, G_QK), row),
                  pl.BlockSpec((1, tg, G_WIDTH), row), pl.BlockSpec((1, tg, G_WIDTH), row),
                  pl.BlockSpec((1, tg, G_QK), row), pl.BlockSpec(gn.shape, lambda bi, ti: (0, 0)), st_spec],
        out_specs=[pl.BlockSpec((1, tg, G_WIDTH), row), st_spec],
        out_shape=[jax.ShapeDtypeStruct((b, t, G_WIDTH), BF16), jax.ShapeDtypeStruct((b, G_WIDTH, G_QK), F32)],
        scratch_shapes=[pltpu.VMEM((G_WIDTH, G_QK), F32)],
        compiler_params=_cparams(("arbitrary", "arbitrary")),
        name="gla",
    )(gq, gk, gv, gr, la, gn, st0)


def _merge_kernel(x_ref, ao_ref, go_ref, lo_ref, nm_ref, wm_ref, bm_ref, wba_ref, wbg_ref, wbl_ref, wo_ref, o_ref):
    x = x_ref[...]
    xb = _rms(x, nm_ref[...]).astype(BF16)
    g = _sigmoid(_dot(xb, wm_ref[...]) + bm_ref[...])
    merged = (g[:, :D_MODEL] * _dot(ao_ref[...], wba_ref[...])
              + g[:, D_MODEL:2 * D_MODEL] * _dot(go_ref[...], wbg_ref[...])
              + g[:, 2 * D_MODEL:] * _dot(lo_ref[...], wbl_ref[...]))
    o_ref[...] = x + _dot(merged.astype(BF16), wo_ref[...])


def _merge_call(x2, ao, go, lo, w):
    n, d = x2.shape
    tm = min(n, 512)
    assert n % tm == 0
    row = lambda i: (i, 0)
    weights = (w['norm_mix'], w['w_merge'], w['b_merge'], w['w_branch_attn'], w['w_branch_gla'], w['w_branch_lru'],
               w['w_out'])
    wspecs = [pl.BlockSpec(a.shape, lambda i: (0, 0), pipeline_mode=pl.Buffered(1)) for a in weights]
    return pl.pallas_call(
        _merge_kernel,
        grid=(n // tm,),
        in_specs=[pl.BlockSpec((tm, d), row), pl.BlockSpec((tm, A_WIDTH), row), pl.BlockSpec((tm, G_WIDTH), row),
                  pl.BlockSpec((tm, LRU_WIDTH), row)] + wspecs,
        out_specs=pl.BlockSpec((tm, d), row),
        out_shape=jax.ShapeDtypeStruct((n, d), F32),
        compiler_params=_cparams(("arbitrary",)),
        name="merge",
    )(x2, ao, go, lo, *weights)


def _ffn_kernel(x_ref, nf_ref, wg_ref, cw_ref, cb_ref, wu_ref, wd_ref, nl_ref, buf0_ref, o_ref, bufo_ref, ubuf,
                *, seqs, tm, final):
    t = pl.program_id(1)
    x = x_ref[...].reshape(seqs * tm, D_MODEL)
    hb = _rms(x, nf_ref[...]).astype(BF16)

    @pl.when(t == 0)
    def _():
        ubuf[...] = buf0_ref[...]

    gu = _dot(hb, wg_ref[...])
    gc = _causal_conv(gu, ubuf[...], cw_ref[...], cb_ref[...])
    tail = _last_groups(gu, seqs)
    bufo_ref[...] = tail
    ubuf[...] = tail
    f = _gelu_tanh(gc) * _dot(hb, wu_ref[...])
    y = x + _dot(f.astype(BF16), wd_ref[...])
    y = _rms(y, nl_ref[...]) if final else y
    o_ref[...] = y.reshape(seqs, tm, D_MODEL)


def _ffn_call(x, w, norm_last, buf0, final):
    b, t, d = x.shape
    tm = min(t, 512)
    assert t % tm == 0
    seqs = max(1, min(b, 256 // tm)) if tm == t else 1
    assert b % seqs == 0
    row = lambda bi, ti: (bi, ti, 0)
    const2 = lambda bi, ti: (0, 0)
    buf_spec = pl.BlockSpec((seqs, CARRY_ROWS, D_FF), lambda bi, ti: (bi, 0, 0))
    weights = (w['norm_ffn'], w['w_ffn_gate'], w['ffn_conv_w'], w['ffn_conv_b'], w['w_ffn_up'], w['w_ffn_down'],
               norm_last)
    wspecs = [pl.BlockSpec(a.shape, const2, pipeline_mode=pl.Buffered(1)) for a in weights]
    return pl.pallas_call(
        functools.partial(_ffn_kernel, seqs=seqs, tm=tm, final=final),
        grid=(b // seqs, t // tm),
        in_specs=[pl.BlockSpec((seqs, tm, d), row)] + wspecs + [buf_spec],
        out_specs=[pl.BlockSpec((seqs, tm, d), row), buf_spec],
        out_shape=[jax.ShapeDtypeStruct((b, t, d), F32), jax.ShapeDtypeStruct((b, CARRY_ROWS, D_FF), F32)],
        scratch_shapes=[pltpu.VMEM((seqs, CARRY_ROWS, D_FF), F32)],
        compiler_params=_cparams(("arbitrary", "arbitrary")),
        name="ffn",
    )(x, *weights, buf0)


def _rope_tables(p_len, t):
    half = ROPE_DIM // 2
    pos = (p_len + jnp.arange(t, dtype=jnp.int32)).astype(F32)
    inv = ROPE_THETA ** (-jnp.arange(half, dtype=F32) / half)
    ang = pos[:, None] * inv[None, :]
    cos, sin = jnp.cos(ang), jnp.sin(ang)
    rest = A_HEAD_DIM - ROPE_DIM
    z = jnp.zeros((t, half), F32)
    c64 = jnp.concatenate([cos, cos, jnp.ones((t, rest), F32)], axis=1)
    sa64 = jnp.concatenate([-sin, z, jnp.zeros((t, rest), F32)], axis=1)
    sb64 = jnp.concatenate([z, sin, jnp.zeros((t, rest), F32)], axis=1)
    rep = LANES // A_HEAD_DIM
    return tuple(jnp.tile(a, (1, rep)) for a in (c64, sa64, sb64))


def _block_diag(wb):
    n, bi, bo = wb.shape
    eye = jnp.eye(n, dtype=wb.dtype)
    return (eye[:, None, :, None] * wb[:, :, None, :]).reshape(n * bi, n * bo)


def _prep_layer(l, p):
    aw = 2 * A_HEADS * A_HEAD_DIM
    o_gla = 2 * aw + A_WIDTH
    o_ga = o_gla + 2 * G_QK + 2 * G_WIDTH
    o_lru = o_ga + G_GATE_RANK
    w_in = p['w_in'][l]
    row = lambda a: a[l].reshape(1, -1)
    return {
        'norm_mix': row(p['norm_mix']),
        'wqkv': w_in[:, :o_gla].astype(BF16),
        'wgla': w_in[:, o_gla:o_ga].astype(BF16),
        'wga': w_in[:, o_ga:o_lru].astype(BF16),
        'wlru': w_in[:, o_lru:].astype(BF16),
        'wg2': p['w_gla_gate2'][l].astype(BF16),
        'bg2': row(p['b_gla_gate']),
        'lambda_qk': p['lambda_qk'][l],
        'attn_subln': row(p['attn_subln']),
        'gla_norm': row(p['gla_norm']),
        'lru_conv_w': p['lru_conv_w'][l],
        'lru_conv_b': row(p['lru_conv_b']),
        'wa_bd': _block_diag(p['lru_wa'][l]).astype(BF16),
        'lru_ba': row(p['lru_ba']),
        'wx_bd': _block_diag(p['lru_wx'][l]).astype(BF16),
        'lru_bx': row(p['lru_bx']),
        'lru_lambda': row(p['lru_lambda']),
        'w_branch_attn': p['w_branch_attn'][l].astype(BF16),
        'w_branch_gla': p['w_branch_gla'][l].astype(BF16),
        'w_branch_lru': p['w_branch_lru'][l].astype(BF16),
        'w_merge': p['w_merge'][l].astype(BF16),
        'b_merge': row(p['b_merge']),
        'w_out': p['w_out'][l].astype(BF16),
        'norm_ffn': row(p['norm_ffn']),
        'w_ffn_gate': p['w_ffn_gate'][l].astype(BF16),
        'ffn_conv_w': p['ffn_conv_w'][l],
        'ffn_conv_b': row(p['ffn_conv_b']),
        'w_ffn_up': p['w_ffn_up'][l].astype(BF16),
        'w_ffn_down': p['w_ffn_down'][l].astype(BF16),
    }


def _pad_carry(buf):
    return jnp.pad(buf, ((0, 0), (CARRY_ROWS - buf.shape[1], 0), (0, 0)))


def _state_to_kernel(s):
    b = s.shape[0]
    eye = jnp.eye(G_HEADS, dtype=s.dtype)
    st = jnp.swapaxes(s, 2, 3)
    return (st[:, :, :, None, :] * eye[None, :, None, :, None]).reshape(b, G_WIDTH, G_QK)


def _state_from_kernel(st):
    b = st.shape[0]
    s5 = st.reshape(b, G_HEADS, G_DV, G_HEADS, G_DK)
    diag = jnp.stack([s5[:, h, :, h, :] for h in range(G_HEADS)], axis=1)
    return jnp.swapaxes(diag, 2, 3)


def _trunk(x, caches, layers, norm_final):
    cache_k, cache_v, st_gla, st_lconv, st_lh, st_fconv = caches
    b, t, d = x.shape
    p_len = 0 if cache_k is None else cache_k.shape[2]
    tabs = _rope_tables(p_len, t)
    if cache_k is not None:
        depth = cache_k.shape[0]
        cache_kt = jnp.transpose(cache_k, (0, 1, 3, 4, 5, 2)).reshape(depth, b, -1, p_len)
        cache_vr = cache_v.reshape(depth, b, p_len * A_HEADS, A_VDIM)
    outs = [[], [], [], [], [], []]
    stacks = None
    for l, w in enumerate(layers):
        lam_init = 0.8 - 0.6 * math.exp(-0.3 * l)
        (kt_all, vr_all, qb, kb, vt, gq, gk, gv, gr, la, lo, lbuf, hl) = _proj_call(
            x, tabs, w, _pad_carry(st_lconv[l]), st_lh[l].reshape(b, 1, LRU_WIDTH), l, len(layers), stacks)
        stacks = (kt_all, vr_all)
        if cache_k is None:
            ao = _attn_prompt_call(w['lambda_qk'], w['attn_subln'], qb, kb, vt, lam_init)
        else:
            ao = _attn_cached_call(w['lambda_qk'], w['attn_subln'], qb, kb, vr_all, cache_kt, cache_vr, l, lam_init)
        go, st_new = _gla_call(gq, gk, gv, gr, la, w['gla_norm'], _state_to_kernel(st_gla[l]))
        x1 = _merge_call(x.reshape(b * t, d), ao.reshape(b * t, -1), go.reshape(b * t, -1), lo.reshape(b * t, -1), w)
        x, fbuf = _ffn_call(x1.reshape(b, t, d), w, norm_final.reshape(1, -1), _pad_carry(st_fconv[l]),
                            final=(l == len(layers) - 1))
        outs[2].append(_state_from_kernel(st_new))
        outs[3].append(lbuf[:, CARRY_ROWS - (LRU_CONV - 1):])
        outs[4].append(hl.reshape(b, LRU_WIDTH))
        outs[5].append(fbuf[:, CARRY_ROWS - (FFN_CONV - 1):])
    kt_all, vr_all = stacks
    depth = len(layers)
    new_k = jnp.transpose(kt_all.reshape(depth, b, A_HEADS, 2, A_HEAD_DIM, t), (0, 1, 5, 2, 3, 4))
    new_v = vr_all.reshape(depth, b, t, A_HEADS, A_VDIM)
    return x, [new_k, new_v] + [jnp.stack(o) for o in outs[2:]]


@jax.jit
def _forward(x_prompt, x_sample, cache_attn_k, cache_attn_v, state_gla, state_lru_conv, state_lru_h, state_ffn_conv,
             params, norm_final):
    layers = [_prep_layer(l, params) for l in range(DEPTH)]
    bp = x_prompt.shape[0]
    zeros = (None, None,
             jnp.zeros((DEPTH, bp, G_HEADS, G_DK, G_DV), F32),
             jnp.zeros((DEPTH, bp, LRU_CONV - 1, LRU_WIDTH), F32),
             jnp.zeros((DEPTH, bp, LRU_WIDTH), F32),
             jnp.zeros((DEPTH, bp, FFN_CONV - 1, D_FF), F32))
    y_p, new_p = _trunk(x_prompt, zeros, layers, norm_final)
    y_s, new_s = _trunk(x_sample, (cache_attn_k, cache_attn_v, state_gla, state_lru_conv, state_lru_h,
                                   state_ffn_conv), layers, norm_final)
    return (y_p, y_s, *new_p, *new_s)


def kernel(x_prompt, x_sample, cache_attn_k, cache_attn_v, state_gla, state_lru_conv, state_lru_h, state_ffn_conv, norm_mix, w_in, lambda_qk, attn_subln, w_gla_gate2, b_gla_gate, gla_norm, lru_conv_w, lru_conv_b, lru_wa, lru_ba, lru_wx, lru_bx, lru_lambda, w_branch_attn, w_branch_gla, w_branch_lru, w_merge, b_merge, w_out, norm_ffn, w_ffn_gate, ffn_conv_w, ffn_conv_b, w_ffn_up, w_ffn_down, norm_final):
    params = dict(norm_mix=norm_mix, w_in=w_in, lambda_qk=lambda_qk, attn_subln=attn_subln, w_gla_gate2=w_gla_gate2,
                  b_gla_gate=b_gla_gate, gla_norm=gla_norm, lru_conv_w=lru_conv_w, lru_conv_b=lru_conv_b,
                  lru_wa=lru_wa, lru_ba=lru_ba, lru_wx=lru_wx, lru_bx=lru_bx, lru_lambda=lru_lambda,
                  w_branch_attn=w_branch_attn, w_branch_gla=w_branch_gla, w_branch_lru=w_branch_lru,
                  w_merge=w_merge, b_merge=b_merge, w_out=w_out, norm_ffn=norm_ffn, w_ffn_gate=w_ffn_gate,
                  ffn_conv_w=ffn_conv_w, ffn_conv_b=ffn_conv_b, w_ffn_up=w_ffn_up, w_ffn_down=w_ffn_down)
    return _forward(x_prompt, x_sample, cache_attn_k, cache_attn_v, state_gla, state_lru_conv, state_lru_h,
                    state_ffn_conv, params, norm_final)
```

```python
import functools
import math

import jax
import jax.numpy as jnp
from jax import lax
from jax.experimental import pallas as pl
from jax.experimental.pallas import tpu as pltpu

F32 = jnp.float32
BF16 = jnp.bfloat16

D_MODEL = 1024
DEPTH = 4
CHUNK = 64
EPS = 1e-6
A_HEADS = 4
A_HEAD_DIM = 64
A_VDIM = 128
A_WIDTH = A_HEADS * A_VDIM
ROPE_DIM = 16
ROPE_THETA = 500000.0
G_HEADS = 4
G_DK = 64
G_DV = 128
G_QK = G_HEADS * G_DK
G_WIDTH = G_HEADS * G_DV
G_GATE_RANK = 16
G_GATE_TAU = 16.0
LRU_WIDTH = 512
LRU_BLOCKS = 8
LRU_CONV = 4
LRU_C = 8.0
D_FF = 2816
FFN_CONV = 3

LANES = 128
SUBLANES = 8
BF16_SUBLANES = 16
LOG2E = math.log2(math.e)
CARRY_ROWS = 8
NEG_BIG = -1e30
VMEM_LIMIT = 56 * 1024 * 1024


def _cparams(sem):
    return pltpu.CompilerParams(dimension_semantics=sem, vmem_limit_bytes=VMEM_LIMIT)


def _rms(x, g):
    return x * lax.rsqrt(jnp.mean(x * x, axis=-1, keepdims=True) + EPS) * g


def _sigmoid(x):
    return 0.5 * jnp.tanh(0.5 * x) + 0.5


def _sqrt_bounded(x):
    return jnp.where(x == 0.0, 0.0, x * lax.rsqrt(x))


def _softplus(x):
    return jnp.maximum(x, 0.0) + jnp.log1p(jnp.exp(-jnp.abs(x)))


def _gelu_tanh(x):
    return x * (0.5 * (1.0 + jnp.tanh(math.sqrt(2.0 / math.pi) * (x + 0.044715 * (x * x * x)))))


def _dot(a, b):
    return jnp.dot(a, b, preferred_element_type=F32)


def _dot_nt(a, b):
    return lax.dot_general(a, b, (((1,), (1,)), ((), ())), preferred_element_type=F32)


def _dot_tn(a, b):
    return lax.dot_general(a, b, (((0,), (0,)), ((), ())), preferred_element_type=F32)


def _lane_tile(x, n):
    return x if n == 1 else jnp.concatenate([x] * n, axis=1)


def _rope(x, c, sa, sb):
    segs = []
    for g in range(x.shape[1] // LANES):
        seg = x[:, g * LANES:(g + 1) * LANES]
        seg_up = pltpu.roll(seg, LANES - ROPE_DIM // 2, 1)
        seg_dn = pltpu.roll(seg, ROPE_DIM // 2, 1)
        segs.append(seg * c + seg_up * sa + seg_dn * sb)
    return jnp.concatenate(segs, axis=1)


def _causal_conv(x, prev, w, bias):
    seqs = prev.shape[0]
    rows, width = x.shape
    taps = w.shape[0]
    gps = rows // seqs // SUBLANES
    x3 = x.reshape(seqs * gps, SUBLANES, width)
    ext = []
    for s in range(seqs):
        ext += [prev[s:s + 1], x3[s * gps:(s + 1) * gps]]
    ext = jnp.concatenate(ext, axis=0)

    def pick(r, first):
        parts = [r[s * (gps + 1) + first:s * (gps + 1) + first + gps] for s in range(seqs)]
        return parts[0] if seqs == 1 else jnp.concatenate(parts, axis=0)

    sub = lax.broadcasted_iota(jnp.int32, x3.shape, 1)
    y = bias + x3 * w[taps - 1:taps]
    for d in range(1, taps):
        r = pltpu.roll(ext, d, 1)
        y = y + jnp.where(sub >= d, pick(r, 1), pick(r, 0)) * w[taps - 1 - d:taps - d]
    return y.reshape(rows, width)


def _last_groups(x, seqs):
    per = x.shape[0] // seqs
    tails = [x[(s + 1) * per - SUBLANES:(s + 1) * per] for s in range(seqs)]
    return jnp.stack(tails, axis=0)


def _lin_scan(a, u, h_in, h_ref):
    tm, width = a.shape
    groups = tm // SUBLANES
    a = a.reshape(groups, SUBLANES, width)
    u = u.reshape(groups, SUBLANES, width)
    sub = lax.broadcasted_iota(jnp.int32, a.shape, 1)
    d = 1
    while d < SUBLANES:
        a_s = pltpu.roll(a, d, 1)
        u_s = pltpu.roll(u, d, 1)
        valid = sub >= d
        u = jnp.where(valid, a * u_s + u, u)
        a = jnp.where(valid, a * a_s, a)
        d *= 2
    carry = h_in
    for g in range(groups):
        h_ref[g * SUBLANES:(g + 1) * SUBLANES] = a[g] * carry + u[g]
        carry = h_ref[(g + 1) * SUBLANES - 1:(g + 1) * SUBLANES]


def _proj_kernel(x_ref, cos_ref, sa_ref, sb_ref, nm_ref, wqkv_ref, wgla_ref, wga_ref, wg2_ref, bg2_ref, wlru_ref,
                 cw_ref, cb_ref, wa_ref, ba_ref, wx_ref, bx_ref, lam_ref, buf0_ref, h0_ref,
                 kt_ref, vr_ref, qb_ref, kb_ref, vt_ref, gq_ref, gk_ref, gv_ref, gr_ref, la_ref, lo_ref,
                 bufo_ref, hl_ref, xbuf, hc, hbuf, *, tm):
    t = pl.program_id(1)
    xn = _rms(x_ref[0], nm_ref[...])
    xb = xn.astype(BF16)

    lx = _dot(xb, wlru_ref[:, :LRU_WIDTH])
    lg = _dot(xb, wlru_ref[:, LRU_WIDTH:])

    @pl.when(t == 0)
    def _():
        xbuf[...] = buf0_ref[...]
        hc[...] = h0_ref[0]

    xc = _causal_conv(lx, xbuf[...], cw_ref[...], cb_ref[...])
    tail = _last_groups(lx, 1)
    bufo_ref[...] = tail
    xbuf[...] = tail

    xcb = xc.astype(BF16)
    r_pre = _dot(xcb, wa_ref[...]) + ba_ref[...]
    i_pre = _dot(xcb, wx_ref[...]) + bx_ref[...]

    c, sa, sb = cos_ref[...], sa_ref[...], sb_ref[...]
    aw = 2 * A_HEADS * A_HEAD_DIM
    q = _rope(_dot(xb, wqkv_ref[:, :aw]), c, sa, sb) * (A_HEAD_DIM ** -0.5 * LOG2E)
    k = _rope(_dot(xb, wqkv_ref[:, aw:2 * aw]), c, sa, sb)
    v = _dot(xb, wqkv_ref[:, 2 * aw:])
    kt_ref[0, 0] = k.T
    for h in range(A_HEADS):
        vr_ref[0, 0, pl.ds(h, tm, stride=A_HEADS), :] = v[:, h * A_VDIM:(h + 1) * A_VDIM]
    kb_ref[0] = k.astype(BF16)
    vt_ref[0, :, 0] = v.T.reshape(A_HEADS, A_VDIM, tm).astype(BF16)
    lane = lax.broadcasted_iota(jnp.int32, (tm, LANES), 1)
    for h in range(A_HEADS):
        seg = q[:, h * LANES:(h + 1) * LANES]
        qb_ref[0, :, 2 * h * LANES:(2 * h + 1) * LANES] = jnp.where(lane < A_HEAD_DIM, seg, 0.0).astype(BF16)
        qb_ref[0, :, (2 * h + 1) * LANES:(2 * h + 2) * LANES] = jnp.where(lane >= A_HEAD_DIM, seg, 0.0).astype(BF16)

    gq_ref[0] = _dot(xb, wgla_ref[:, :G_QK]) * (G_DK ** -0.5)
    gk_ref[0] = _dot(xb, wgla_ref[:, G_QK:2 * G_QK])
    gv_ref[0] = _dot(xb, wgla_ref[:, 2 * G_QK:2 * G_QK + G_WIDTH]).astype(BF16)
    gr_ref[0] = _dot(xb, wgla_ref[:, 2 * G_QK + G_WIDTH:])
    ga = _dot(xb, wga_ref[...])
    gate = _dot(ga.astype(BF16), wg2_ref[...]) + bg2_ref[...]
    la_ref[0] = -_softplus(-gate) * (1.0 / G_GATE_TAU)

    log_a = (-LRU_C) * _sigmoid(r_pre) * _softplus(-lam_ref[...])
    a = jnp.exp(log_a)
    u = _sqrt_bounded(-jnp.tanh(log_a) * (a * a + 1.0)) * (_sigmoid(i_pre) * xc)
    _lin_scan(a, u, hc[...], hbuf)
    h_last = hbuf[tm - 1:tm]
    hc[...] = h_last
    hl_ref[0] = h_last
    lo_ref[0] = (hbuf[...] * _gelu_tanh(lg)).astype(BF16)


def _proj_kernel_stacked(kst_ref, vst_ref, *refs, tm):
    del kst_ref, vst_ref
    _proj_kernel(*refs, tm=tm)


def _proj_call(x, tabs, w, buf0, h0, layer, depth, stacks):
    b, t, d = x.shape
    tm = min(t, 512)
    assert t % tm == 0 and tm % 8 == 0
    nt = t // tm
    row = lambda bi, ti: (bi, ti, 0)
    const2 = lambda bi, ti: (0, 0)
    tab_spec = pl.BlockSpec((tm, LANES), lambda bi, ti: (ti, 0))

    def wspec(a):
        return pl.BlockSpec(a.shape, const2, pipeline_mode=pl.Buffered(1))

    weights = (w['norm_mix'], w['wqkv'], w['wgla'], w['wga'], w['wg2'], w['bg2'], w['wlru'], w['lru_conv_w'],
               w['lru_conv_b'], w['wa_bd'], w['lru_ba'], w['wx_bd'], w['lru_bx'], w['lru_lambda'])
    in_specs = ([pl.BlockSpec((1, tm, d), row), tab_spec, tab_spec, tab_spec] + [wspec(a) for a in weights]
                + [pl.BlockSpec((1, CARRY_ROWS, LRU_WIDTH), lambda bi, ti: (bi, 0, 0)),
                   pl.BlockSpec((1, 1, LRU_WIDTH), lambda bi, ti: (bi, 0, 0))])
    aw = 2 * A_HEADS * A_HEAD_DIM

    def out(wd, dt):
        return jax.ShapeDtypeStruct((b, t, wd), dt), pl.BlockSpec((1, tm, wd), row)

    vt_out = (jax.ShapeDtypeStruct((b, A_HEADS, nt, A_VDIM, tm), BF16),
              pl.BlockSpec((1, A_HEADS, 1, A_VDIM, tm), lambda bi, ti: (bi, 0, ti, 0, 0)))
    kt_out = (jax.ShapeDtypeStruct((depth, b, aw, t), F32),
              pl.BlockSpec((1, 1, aw, tm), lambda bi, ti: (layer, bi, 0, ti)))
    vr_out = (jax.ShapeDtypeStruct((depth, b, t * A_HEADS, A_VDIM), F32),
              pl.BlockSpec((1, 1, tm * A_HEADS, A_VDIM), lambda bi, ti: (layer, bi, ti, 0)))
    outs = [kt_out, vr_out, out(2 * aw, BF16), out(aw, BF16), vt_out,
            out(G_QK, F32), out(G_QK, F32), out(G_WIDTH, BF16), out(G_WIDTH, F32), out(G_QK, F32),
            out(LRU_WIDTH, BF16),
            (jax.ShapeDtypeStruct((b, CARRY_ROWS, LRU_WIDTH), F32),
             pl.BlockSpec((1, CARRY_ROWS, LRU_WIDTH), lambda bi, ti: (bi, 0, 0))),
            (jax.ShapeDtypeStruct((b, 1, LRU_WIDTH), F32), pl.BlockSpec((1, 1, LRU_WIDTH), lambda bi, ti: (bi, 0, 0)))]
    args = (x, *tabs, *weights, buf0, h0)
    body, aliases = _proj_kernel, {}
    if stacks is not None:
        args = (*stacks, *args)
        in_specs = [pl.BlockSpec(memory_space=pl.ANY)] * 2 + in_specs
        body, aliases = _proj_kernel_stacked, {0: 0, 1: 1}
    return pl.pallas_call(
        functools.partial(body, tm=tm),
        grid=(b, nt),
        in_specs=in_specs,
        out_specs=[o[1] for o in outs],
        out_shape=[o[0] for o in outs],
        input_output_aliases=aliases,
        scratch_shapes=[pltpu.VMEM((1, CARRY_ROWS, LRU_WIDTH), F32), pltpu.VMEM((1, LRU_WIDTH), F32),
                        pltpu.VMEM((tm, LRU_WIDTH), F32)],
        compiler_params=_cparams(("arbitrary", "arbitrary")),
        name="proj",
    )(*args)


def _attn_lambda(lq_ref, lam_init):
    lq = lq_ref[...]
    s1 = jnp.sum(lq[0:1] * lq[1:2], axis=1, keepdims=True)
    s2 = jnp.sum(lq[2:3] * lq[3:4], axis=1, keepdims=True)
    return jnp.exp(s1) - jnp.exp(s2) + lam_init


def _attn_norm(o, sub, lam_init):
    ms = jnp.mean(o * o, axis=-1, keepdims=True)
    return (o * lax.rsqrt(ms + EPS) * sub * (1.0 - lam_init)).astype(BF16)


def _chunk_mask(q0, k0, tq, tk, keys_on_rows=False):
    shape = (tk, tq) if keys_on_rows else (tq, tk)
    qpos = q0 + lax.broadcasted_iota(jnp.int32, shape, 1 if keys_on_rows else 0)
    kpos = k0 + lax.broadcasted_iota(jnp.int32, shape, 0 if keys_on_rows else 1)
    return (kpos // CHUNK) <= (qpos // CHUNK)


def _attn_prompt_kernel(lq_ref, sub_ref, q_ref, k_ref, vt_ref, o_ref, m_sc, acc_sc, s_sc, mt_sc,
                        *, tq, tk, tv, lam_init):
    i = pl.program_id(2)
    m_sc[...] = jnp.full(m_sc.shape, NEG_BIG, F32)
    acc_sc[...] = jnp.zeros(acc_sc.shape, F32)
    q = q_ref[0]
    qs = (q[:, :LANES], q[:, LANES:])
    ones = jnp.ones((BF16_SUBLANES, tk), BF16)

    def scores(t, slot, masked):
        k = k_ref[0, pl.ds(pl.multiple_of(t * tk, tk), tk), :]
        for c in range(2):
            s = _dot_nt(k, qs[c])
            if masked:
                s = jnp.where(_chunk_mask(i * tq, t * tk, tq, tk, keys_on_rows=True), s, NEG_BIG)
            s_sc[slot, c] = s
            mt_sc[slot, c] = jnp.max(s, axis=0, keepdims=True)

    def consume(t, slot):
        vt = [vt_ref[0, 0, t * (tk // tv) + u] for u in range(tk // tv)]
        vt_aug = jnp.concatenate([_lane_cat(vt), ones], axis=0)
        for c in range(2):
            m_prev = m_sc[c]
            m_new = jnp.maximum(m_prev, mt_sc[slot, c])
            alpha = jnp.exp2(m_prev - m_new)
            p = jnp.exp2(s_sc[slot, c] - m_new).astype(BF16)
            acc_sc[c] = acc_sc[c] * alpha + _dot(vt_aug, p)
            m_sc[c] = m_new

    n_diag = tq // tk
    n_full = i * n_diag
    n_plain = jnp.maximum(n_full - 1, 0)
    scores(0, 0, True)

    def pair(p, carry):
        t = 2 * p
        scores(t + 1, 1, False)
        consume(t, 0)
        scores(t + 2, 0, False)
        consume(t + 1, 1)
        return carry

    lax.fori_loop(0, n_plain // 2, pair, 0)
    tc = (n_plain // 2) * 2

    def tail(n_unmasked, n_masked):
        count = n_unmasked + n_masked
        for u in range(count - 1):
            scores(tc + u + 1, (u + 1) % 2, u + 1 >= n_unmasked)
            consume(tc + u, u % 2)
        consume(tc + count - 1, (count - 1) % 2)

    @pl.when(n_full == 0)
    def _():
        tail(0, n_diag)

    @pl.when(jnp.logical_and(n_full > 0, n_plain % 2 == 0))
    def _():
        tail(1, n_diag)

    @pl.when(jnp.logical_and(n_full > 0, n_plain % 2 == 1))
    def _():
        tail(2, n_diag)

    a0, a1 = acc_sc[0], acc_sc[1]
    lam = _attn_lambda(lq_ref, lam_init)
    o_t = a0[:A_VDIM] / a0[A_VDIM:A_VDIM + 1] - lam * (a1[:A_VDIM] / a1[A_VDIM:A_VDIM + 1])
    o_ref[0] = _attn_norm(o_t.T, sub_ref[...], lam_init)


def _lane_cat(parts):
    return parts[0] if len(parts) == 1 else jnp.concatenate(parts, axis=1)


def _attn_prompt_call(lq, sub, qb, kb, vt, lam_init):
    b, t, _ = kb.shape
    tv = vt.shape[-1]
    tq = min(t, 1024)
    tk = min(t, 512)
    assert t % tq == 0 and (tq % CHUNK == 0 or tq == t) and tq % LANES == 0 and tk % tv == 0 and tq % tk == 0
    return pl.pallas_call(
        functools.partial(_attn_prompt_kernel, tq=tq, tk=tk, tv=tv, lam_init=lam_init),
        grid=(b, A_HEADS, t // tq),
        in_specs=[pl.BlockSpec(lq.shape, lambda bi, h, i: (0, 0)),
                  pl.BlockSpec(sub.shape, lambda bi, h, i: (0, 0)),
                  pl.BlockSpec((1, tq, 2 * LANES), lambda bi, h, i: (bi, i, h)),
                  pl.BlockSpec((1, t, LANES), lambda bi, h, i: (bi, 0, h)),
                  pl.BlockSpec((1, 1, t // tv, A_VDIM, tv), lambda bi, h, i: (bi, h, 0, 0, 0))],
        out_specs=pl.BlockSpec((1, tq, LANES), lambda bi, h, i: (bi, i, h)),
        out_shape=jax.ShapeDtypeStruct((b, t, A_WIDTH), BF16),
        scratch_shapes=[pltpu.VMEM((2, 1, tq), F32), pltpu.VMEM((2, A_VDIM + BF16_SUBLANES, tq), F32),
                        pltpu.VMEM((2, 2, tk, tq), F32), pltpu.VMEM((2, 2, 1, tq), F32)],
        compiler_params=_cparams(("arbitrary", "arbitrary", "arbitrary")),
        name="attn_prompt",
    )(lq, sub, qb, kb, vt)


def _attn_cached_kernel(lq_ref, sub_ref, q_ref, kpt_ref, vp_ref, kn_ref, vn_ref, o_ref, *, t, p_len, lam_init):
    mask_p = jnp.concatenate([_chunk_mask(p_len, 0, t, p_len)] * 2, axis=0)
    mask_n = jnp.concatenate([_chunk_mask(p_len, p_len, t, t)] * 2, axis=0)
    lam = _attn_lambda(lq_ref, lam_init)
    ones_p = jnp.ones((p_len, LANES), BF16)
    ones_n = jnp.ones((t, LANES), BF16)
    for h in range(A_HEADS):
        cols = slice(h * LANES, (h + 1) * LANES)
        kpt = kpt_ref[0, 0, cols, :].astype(BF16)
        kn = kn_ref[0, :, cols]
        vp = jnp.concatenate([vp_ref[0, 0, pl.ds(h, p_len, stride=A_HEADS), :].astype(BF16), ones_p], axis=1)
        vn = jnp.concatenate([vn_ref[0, 0, pl.ds(h, t, stride=A_HEADS), :].astype(BF16), ones_n], axis=1)
        q2 = jnp.concatenate([q_ref[0, :, 2 * h * LANES:(2 * h + 1) * LANES],
                              q_ref[0, :, (2 * h + 1) * LANES:(2 * h + 2) * LANES]], axis=0)
        s_p = jnp.where(mask_p, _dot(q2, kpt), NEG_BIG)
        s_n = jnp.where(mask_n, _dot_nt(q2, kn), NEG_BIG)
        m = jnp.maximum(jnp.max(s_p, axis=1, keepdims=True), jnp.max(s_n, axis=1, keepdims=True))
        acc = _dot(jnp.exp2(s_p - m).astype(BF16), vp) + _dot(jnp.exp2(s_n - m).astype(BF16), vn)
        ratio = acc[:, :LANES] / acc[:, LANES:]
        o = ratio[:t] - lam * ratio[t:]
        o_ref[0, :, cols] = _attn_norm(o, sub_ref[...], lam_init)


def _attn_cached_call(lq, sub, qb, k_new, v_new_rows, k_past_t, v_past_rows, layer, lam_init):
    b, t, _ = k_new.shape
    p_len = k_past_t.shape[-1]
    aw = 2 * A_HEADS * A_HEAD_DIM
    return pl.pallas_call(
        functools.partial(_attn_cached_kernel, t=t, p_len=p_len, lam_init=lam_init),
        grid=(b,),
        in_specs=[pl.BlockSpec(lq.shape, lambda bi: (0, 0)),
                  pl.BlockSpec(sub.shape, lambda bi: (0, 0)),
                  pl.BlockSpec((1, t, 2 * aw), lambda bi: (bi, 0, 0)),
                  pl.BlockSpec((1, 1, aw, p_len), lambda bi: (layer, bi, 0, 0)),
                  pl.BlockSpec((1, 1, p_len * A_HEADS, A_VDIM), lambda bi: (layer, bi, 0, 0)),
                  pl.BlockSpec((1, t, aw), lambda bi: (bi, 0, 0)),
                  pl.BlockSpec((1, 1, t * A_HEADS, A_VDIM), lambda bi: (layer, bi, 0, 0))],
        out_specs=pl.BlockSpec((1, t, A_WIDTH), lambda bi: (bi, 0, 0)),
        out_shape=jax.ShapeDtypeStruct((b, t, A_WIDTH), BF16),
        compiler_params=_cparams(("arbitrary",)),
        name="attn_cached",
    )(lq, sub, qb, k_past_t, v_past_rows, k_new, v_new_rows)


def _split3(x):
    hi = x.astype(BF16)
    r1 = x - hi.astype(F32)
    mid = r1.astype(BF16)
    lo = (r1 - mid.astype(F32)).astype(BF16)
    return hi, mid, lo


def _gla_kernel(q_ref, k_ref, v_ref, r_ref, la_ref, gn_ref, st0_ref, o_ref, sto_ref, st, *, tg, lc):
    t = pl.program_id(1)

    @pl.when(t == 0)
    def _():
        st[...] = st0_ref[0]

    rr = lax.broadcasted_iota(jnp.int32, (lc, lc), 0)
    cc = lax.broadcasted_iota(jnp.int32, (lc, lc), 1)
    tri = rr >= cc
    tri_b = jnp.where(tri, 1.0, 0.0).astype(BF16)
    lane_k = lax.broadcasted_iota(jnp.int32, (lc, G_QK), 1) // G_DK
    bd = (lax.broadcasted_iota(jnp.int32, (G_WIDTH, G_QK), 0) // G_DV
          == lax.broadcasted_iota(jnp.int32, (G_WIDTH, G_QK), 1) // G_DK)
    gn = gn_ref[...]

    per_chunk = []
    for ci in range(tg // lc):
        rows = slice(ci * lc, (ci + 1) * lc)
        hi, mid, lo = _split3(la_ref[0, rows, :])
        bc = _dot(tri_b, hi) + _dot(tri_b, mid) + _dot(tri_b, lo)
        b_mid = bc[lc // 2 - 1:lc // 2]
        b_last = bc[lc - 1:lc]
        q = q_ref[0, rows, :]
        k = k_ref[0, rows, :]
        vb = v_ref[0, rows, :]
        q_in = (q * jnp.exp(bc)).astype(BF16)
        q_mid = q * jnp.exp(bc - b_mid)
        k_mid = (k * jnp.exp(b_mid - bc)).astype(BF16)
        k_end = (k * jnp.exp(b_last - bc)).astype(BF16)
        att = []
        for h in range(G_HEADS):
            qh = jnp.where(lane_k == h, q_mid, 0.0).astype(BF16)
            att.append(jnp.where(tri, _dot_nt(qh, k_mid), 0.0).astype(BF16))
        incr = jnp.where(bd, _dot_tn(vb, k_end), 0.0)
        per_chunk.append((rows, q_in, att, vb, incr, jnp.exp(b_last)))

    intras = [[_dot(att[h], vb[:, h * G_DV:(h + 1) * G_DV]) for h in range(G_HEADS)]
              for _, _, att, vb, _, _ in per_chunk]

    s_cur = st[...]
    for (rows, q_in, _, _, incr, decay), intra in zip(per_chunk, intras):
        inter = _dot_nt(q_in, s_cur.astype(BF16))
        outs = []
        for h in range(G_HEADS):
            oh = inter[:, h * G_DV:(h + 1) * G_DV] + intra[h]
            ms = jnp.mean(oh * oh, axis=-1, keepdims=True)
            outs.append(oh * lax.rsqrt(ms + EPS) * gn)
        gr = r_ref[0, rows, :]
        o_ref[0, rows, :] = (jnp.concatenate(outs, axis=1) * (gr * _sigmoid(gr))).astype(BF16)
        s_cur = s_cur * decay + incr
    st[...] = s_cur
    sto_ref[0] = s_cur


def _gla_call(gq, gk, gv, gr, la, gn, st0):
    b, t, _ = gq.shape
    lc = min(CHUNK, t)
    tg = min(t, 512)
    assert t % tg == 0 and tg % lc == 0 and lc % 16 == 0
    row = lambda bi, ti: (bi, ti, 0)
    st_spec = pl.BlockSpec((1, G_WIDTH, G_QK), lambda bi, ti: (bi, 0, 0))
    return pl.pallas_call(
        functools.partial(_gla_kernel, tg=tg, lc=lc),
        grid=(b, t // tg),
        in_specs=[pl.BlockSpec((1, tg, G_QK), row), pl.BlockSpec((1, tg, G_QK), row),
                  pl.BlockSpec((1, tg, G_WIDTH), row), pl.BlockSpec((1, tg, G_WIDTH), row),
                  pl.BlockSpec((1, tg, G_QK), row), pl.BlockSpec(gn.shape, lambda bi, ti: (0, 0)), st_spec],
        out_specs=[pl.BlockSpec((1, tg, G_WIDTH), row), st_spec],
        out_shape=[jax.ShapeDtypeStruct((b, t, G_WIDTH), BF16), jax.ShapeDtypeStruct((b, G_WIDTH, G_QK), F32)],
        scratch_shapes=[pltpu.VMEM((G_WIDTH, G_QK), F32)],
        compiler_params=_cparams(("arbitrary", "arbitrary")),
        name="gla",
    )(gq, gk, gv, gr, la, gn, st0)


def _merge_kernel(x_ref, ao_ref, go_ref, lo_ref, nm_ref, wm_ref, bm_ref, wba_ref, wbg_ref, wbl_ref, wo_ref, o_ref):
    x = x_ref[...]
    xb = _rms(x, nm_ref[...]).astype(BF16)
    g = _sigmoid(_dot(xb, wm_ref[...]) + bm_ref[...])
    merged = (g[:, :D_MODEL] * _dot(ao_ref[...], wba_ref[...])
              + g[:, D_MODEL:2 * D_MODEL] * _dot(go_ref[...], wbg_ref[...])
              + g[:, 2 * D_MODEL:] * _dot(lo_ref[...], wbl_ref[...]))
    o_ref[...] = x + _dot(merged.astype(BF16), wo_ref[...])


def _merge_call(x2, ao, go, lo, w):
    n, d = x2.shape
    tm = min(n, 512)
    assert n % tm == 0
    row = lambda i: (i, 0)
    weights = (w['norm_mix'], w['w_merge'], w['b_merge'], w['w_branch_attn'], w['w_branch_gla'], w['w_branch_lru'],
               w['w_out'])
    wspecs = [pl.BlockSpec(a.shape, lambda i: (0, 0), pipeline_mode=pl.Buffered(1)) for a in weights]
    return pl.pallas_call(
        _merge_kernel,
        grid=(n // tm,),
        in_specs=[pl.BlockSpec((tm, d), row), pl.BlockSpec((tm, A_WIDTH), row), pl.BlockSpec((tm, G_WIDTH), row),
                  pl.BlockSpec((tm, LRU_WIDTH), row)] + wspecs,
        out_specs=pl.BlockSpec((tm, d), row),
        out_shape=jax.ShapeDtypeStruct((n, d), F32),
        compiler_params=_cparams(("arbitrary",)),
        name="merge",
    )(x2, ao, go, lo, *weights)


def _ffn_kernel(x_ref, nf_ref, wg_ref, cw_ref, cb_ref, wu_ref, wd_ref, nl_ref, buf0_ref, o_ref, bufo_ref, ubuf,
                *, seqs, tm, final):
    t = pl.program_id(1)
    x = x_ref[...].reshape(seqs * tm, D_MODEL)
    hb = _rms(x, nf_ref[...]).astype(BF16)

    @pl.when(t == 0)
    def _():
        ubuf[...] = buf0_ref[...]

    gu = _dot(hb, wg_ref[...])
    gc = _causal_conv(gu, ubuf[...], cw_ref[...], cb_ref[...])
    tail = _last_groups(gu, seqs)
    bufo_ref[...] = tail
    ubuf[...] = tail
    f = _gelu_tanh(gc) * _dot(hb, wu_ref[...])
    y = x + _dot(f.astype(BF16), wd_ref[...])
    y = _rms(y, nl_ref[...]) if final else y
    o_ref[...] = y.reshape(seqs, tm, D_MODEL)


def _ffn_call(x, w, norm_last, buf0, final):
    b, t, d = x.shape
    tm = min(t, 512)
    assert t % tm == 0
    seqs = max(1, min(b, 256 // tm)) if tm == t else 1
    assert b % seqs == 0
    row = lambda bi, ti: (bi, ti, 0)
    const2 = lambda bi, ti: (0, 0)
    buf_spec = pl.BlockSpec((seqs, CARRY_ROWS, D_FF), lambda bi, ti: (bi, 0, 0))
    weights = (w['norm_ffn'], w['w_ffn_gate'], w['ffn_conv_w'], w['ffn_conv_b'], w['w_ffn_up'], w['w_ffn_down'],
               norm_last)
    wspecs = [pl.BlockSpec(a.shape, const2, pipeline_mode=pl.Buffered(1)) for a in weights]
    return pl.pallas_call(
        functools.partial(_ffn_kernel, seqs=seqs, tm=tm, final=final),
        grid=(b // seqs, t // tm),
        in_specs=[pl.BlockSpec((seqs, tm, d), row)] + wspecs + [buf_spec],
        out_specs=[pl.BlockSpec((seqs, tm, d), row), buf_spec],
        out_shape=[jax.ShapeDtypeStruct((b, t, d), F32), jax.ShapeDtypeStruct((b, CARRY_ROWS, D_FF), F32)],
        scratch_shapes=[pltpu.VMEM((seqs, CARRY_ROWS, D_FF), F32)],
        compiler_params=_cparams(("arbitrary", "arbitrary")),
        name="ffn",
    )(x, *weights, buf0)


def _rope_tables(p_len, t):
    half = ROPE_DIM // 2
    pos = (p_len + jnp.arange(t, dtype=jnp.int32)).astype(F32)
    inv = ROPE_THETA ** (-jnp.arange(half, dtype=F32) / half)
    ang = pos[:, None] * inv[None, :]
    cos, sin = jnp.cos(ang), jnp.sin(ang)
    rest = A_HEAD_DIM - ROPE_DIM
    z = jnp.zeros((t, half), F32)
    c64 = jnp.concatenate([cos, cos, jnp.ones((t, rest), F32)], axis=1)
    sa64 = jnp.concatenate([-sin, z, jnp.zeros((t, rest), F32)], axis=1)
    sb64 = jnp.concatenate([z, sin, jnp.zeros((t, rest), F32)], axis=1)
    rep = LANES // A_HEAD_DIM
    return tuple(jnp.tile(a, (1, rep)) for a in (c64, sa64, sb64))


def _block_diag(wb):
    n, bi, bo = wb.shape
    eye = jnp.eye(n, dtype=wb.dtype)
    return (eye[:, None, :, None] * wb[:, :, None, :]).reshape(n * bi, n * bo)


def _prep_layer(l, p):
    aw = 2 * A_HEADS * A_HEAD_DIM
    o_gla = 2 * aw + A_WIDTH
    o_ga = o_gla + 2 * G_QK + 2 * G_WIDTH
    o_lru = o_ga + G_GATE_RANK
    w_in = p['w_in'][l]
    row = lambda a: a[l].reshape(1, -1)
    return {
        'norm_mix': row(p['norm_mix']),
        'wqkv': w_in[:, :o_gla].astype(BF16),
        'wgla': w_in[:, o_gla:o_ga].astype(BF16),
        'wga': w_in[:, o_ga:o_lru].astype(BF16),
        'wlru': w_in[:, o_lru:].astype(BF16),
        'wg2': p['w_gla_gate2'][l].astype(BF16),
        'bg2': row(p['b_gla_gate']),
        'lambda_qk': p['lambda_qk'][l],
        'attn_subln': row(p['attn_subln']),
        'gla_norm': row(p['gla_norm']),
        'lru_conv_w': p['lru_conv_w'][l],
        'lru_conv_b': row(p['lru_conv_b']),
        'wa_bd': _block_diag(p['lru_wa'][l]).astype(BF16),
        'lru_ba': row(p['lru_ba']),
        'wx_bd': _block_diag(p['lru_wx'][l]).astype(BF16),
        'lru_bx': row(p['lru_bx']),
        'lru_lambda': row(p['lru_lambda']),
        'w_branch_attn': p['w_branch_attn'][l].astype(BF16),
        'w_branch_gla': p['w_branch_gla'][l].astype(BF16),
        'w_branch_lru': p['w_branch_lru'][l].astype(BF16),
        'w_merge': p['w_merge'][l].astype(BF16),
        'b_merge': row(p['b_merge']),
        'w_out': p['w_out'][l].astype(BF16),
        'norm_ffn': row(p['norm_ffn']),
        'w_ffn_gate': p['w_ffn_gate'][l].astype(BF16),
        'ffn_conv_w': p['ffn_conv_w'][l],
        'ffn_conv_b': row(p['ffn_conv_b']),
        'w_ffn_up': p['w_ffn_up'][l].astype(BF16),
        'w_ffn_down': p['w_ffn_down'][l].astype(BF16),
    }


def _pad_carry(buf):
    return jnp.pad(buf, ((0, 0), (CARRY_ROWS - buf.shape[1], 0), (0, 0)))


def _state_to_kernel(s):
    b = s.shape[0]
    eye = jnp.eye(G_HEADS, dtype=s.dtype)
    st = jnp.swapaxes(s, 2, 3)
    return (st[:, :, :, None, :] * eye[None, :, None, :, None]).reshape(b, G_WIDTH, G_QK)


def _state_from_kernel(st):
    b = st.shape[0]
    s5 = st.reshape(b, G_HEADS, G_DV, G_HEADS, G_DK)
    diag = jnp.stack([s5[:, h, :, h, :] for h in range(G_HEADS)], axis=1)
    return jnp.swapaxes(diag, 2, 3)


def _trunk(x, caches, layers, norm_final):
    cache_k, cache_v, st_gla, st_lconv, st_lh, st_fconv = caches
    b, t, d = x.shape
    p_len = 0 if cache_k is None else cache_k.shape[2]
    tabs = _rope_tables(p_len, t)
    if cache_k is not None:
        depth = cache_k.shape[0]
        cache_kt = jnp.transpose(cache_k, (0, 1, 3, 4, 5, 2)).reshape(depth, b, -1, p_len)
        cache_vr = cache_v.reshape(depth, b, p_len * A_HEADS, A_VDIM)
    outs = [[], [], [], [], [], []]
    stacks = None
    for l, w in enumerate(layers):
        lam_init = 0.8 - 0.6 * math.exp(-0.3 * l)
        (kt_all, vr_all, qb, kb, vt, gq, gk, gv, gr, la, lo, lbuf, hl) = _proj_call(
            x, tabs, w, _pad_carry(st_lconv[l]), st_lh[l].reshape(b, 1, LRU_WIDTH), l, len(layers), stacks)
        stacks = (kt_all, vr_all)
        if cache_k is None:
            ao = _attn_prompt_call(w['lambda_qk'], w['attn_subln'], qb, kb, vt, lam_init)
        else:
            ao = _attn_cached_call(w['lambda_qk'], w['attn_subln'], qb, kb, vr_all, cache_kt, cache_vr, l, lam_init)
        go, st_new = _gla_call(gq, gk, gv, gr, la, w['gla_norm'], _state_to_kernel(st_gla[l]))
        x1 = _merge_call(x.reshape(b * t, d), ao.reshape(b * t, -1), go.reshape(b * t, -1), lo.reshape(b * t, -1), w)
        x, fbuf = _ffn_call(x1.reshape(b, t, d), w, norm_final.reshape(1, -1), _pad_carry(st_fconv[l]),
                            final=(l == len(layers) - 1))
        outs[2].append(_state_from_kernel(st_new))
        outs[3].append(lbuf[:, CARRY_ROWS - (LRU_CONV - 1):])
        outs[4].append(hl.reshape(b, LRU_WIDTH))
        outs[5].append(fbuf[:, CARRY_ROWS - (FFN_CONV - 1):])
    kt_all, vr_all = stacks
    depth = len(layers)
    new_k = jnp.transpose(kt_all.reshape(depth, b, A_HEADS, 2, A_HEAD_DIM, t), (0, 1, 5, 2, 3, 4))
    new_v = vr_all.reshape(depth, b, t, A_HEADS, A_VDIM)
    return x, [new_k, new_v] + [jnp.stack(o) for o in outs[2:]]


@jax.jit
def _forward(x_prompt, x_sample, cache_attn_k, cache_attn_v, state_gla, state_lru_conv, state_lru_h, state_ffn_conv,
             params, norm_final):
    layers = [_prep_layer(l, params) for l in range(DEPTH)]
    bp = x_prompt.shape[0]
    zeros = (None, None,
             jnp.zeros((DEPTH, bp, G_HEADS, G_DK, G_DV), F32),
             jnp.zeros((DEPTH, bp, LRU_CONV - 1, LRU_WIDTH), F32),
             jnp.zeros((DEPTH, bp, LRU_WIDTH), F32),
             jnp.zeros((DEPTH, bp, FFN_CONV - 1, D_FF), F32))
    y_p, new_p = _trunk(x_prompt, zeros, layers, norm_final)
    y_s, new_s = _trunk(x_sample, (cache_attn_k, cache_attn_v, state_gla, state_lru_conv, state_lru_h,
                                   state_ffn_conv), layers, norm_final)
    return (y_p, y_s, *new_p, *new_s)


def kernel(x_prompt, x_sample, cache_attn_k, cache_attn_v, state_gla, state_lru_conv, state_lru_h, state_ffn_conv, norm_mix, w_in, lambda_qk, attn_subln, w_gla_gate2, b_gla_gate, gla_norm, lru_conv_w, lru_conv_b, lru_wa, lru_ba, lru_wx, lru_bx, lru_lambda, w_branch_attn, w_branch_gla, w_branch_lru, w_merge, b_merge, w_out, norm_ffn, w_ffn_gate, ffn_conv_w, ffn_conv_b, w_ffn_up, w_ffn_down, norm_final):
    params = dict(norm_mix=norm_mix, w_in=w_in, lambda_qk=lambda_qk, attn_subln=attn_subln, w_gla_gate2=w_gla_gate2,
                  b_gla_gate=b_gla_gate, gla_norm=gla_norm, lru_conv_w=lru_conv_w, lru_conv_b=lru_conv_b,
                  lru_wa=lru_wa, lru_ba=lru_ba, lru_wx=lru_wx, lru_bx=lru_bx, lru_lambda=lru_lambda,
                  w_branch_attn=w_branch_attn, w_branch_gla=w_branch_gla, w_branch_lru=w_branch_lru,
                  w_merge=w_merge, b_merge=b_merge, w_out=w_out, norm_ffn=norm_ffn, w_ffn_gate=w_ffn_gate,
                  ffn_conv_w=ffn_conv_w, ffn_conv_b=ffn_conv_b, w_ffn_up=w_ffn_up, w_ffn_down=w_ffn_down)
    return _forward(x_prompt, x_sample, cache_attn_k, cache_attn_v, state_gla, state_lru_conv, state_lru_h,
                    state_ffn_conv, params, norm_final)
```

```python
import functools
import math

import jax
import jax.numpy as jnp
from jax import lax
from jax.experimental import pallas as pl
from jax.experimental.pallas import tpu as pltpu

F32 = jnp.float32
BF16 = jnp.bfloat16

D_MODEL = 1024
DEPTH = 4
CHUNK = 64
EPS = 1e-6
A_HEADS = 4
A_HEAD_DIM = 64
A_VDIM = 128
A_WIDTH = A_HEADS * A_VDIM
ROPE_DIM = 16
ROPE_THETA = 500000.0
G_HEADS = 4
G_DK = 64
G_DV = 128
G_QK = G_HEADS * G_DK
G_WIDTH = G_HEADS * G_DV
G_GATE_RANK = 16
G_GATE_TAU = 16.0
LRU_WIDTH = 512
LRU_BLOCKS = 8
LRU_CONV = 4
LRU_C = 8.0
D_FF = 2816
FFN_CONV = 3

LANES = 128
SUBLANES = 8
BF16_SUBLANES = 16
LOG2E = math.log2(math.e)
CARRY_ROWS = 8
NEG_BIG = -1e30
VMEM_LIMIT = 56 * 1024 * 1024


def _cparams(sem):
    return pltpu.CompilerParams(dimension_semantics=sem, vmem_limit_bytes=VMEM_LIMIT)


def _rms(x, g):
    return x * lax.rsqrt(jnp.mean(x * x, axis=-1, keepdims=True) + EPS) * g


def _sigmoid(x):
    return 0.5 * jnp.tanh(0.5 * x) + 0.5


def _sqrt_bounded(x):
    return jnp.where(x == 0.0, 0.0, x * lax.rsqrt(x))


def _softplus(x):
    return jnp.maximum(x, 0.0) + jnp.log1p(jnp.exp(-jnp.abs(x)))


def _gelu_tanh(x):
    return x * (0.5 * (1.0 + jnp.tanh(math.sqrt(2.0 / math.pi) * (x + 0.044715 * (x * x * x)))))


def _dot(a, b):
    return jnp.dot(a, b, preferred_element_type=F32)


def _dot_nt(a, b):
    return lax.dot_general(a, b, (((1,), (1,)), ((), ())), preferred_element_type=F32)


def _dot_tn(a, b):
    return lax.dot_general(a, b, (((0,), (0,)), ((), ())), preferred_element_type=F32)


def _lane_tile(x, n):
    return x if n == 1 else jnp.concatenate([x] * n, axis=1)


def _rope(x, c, sa, sb):
    segs = []
    for g in range(x.shape[1] // LANES):
        seg = x[:, g * LANES:(g + 1) * LANES]
        seg_up = pltpu.roll(seg, LANES - ROPE_DIM // 2, 1)
        seg_dn = pltpu.roll(seg, ROPE_DIM // 2, 1)
        segs.append(seg * c + seg_up * sa + seg_dn * sb)
    return jnp.concatenate(segs, axis=1)


def _causal_conv(x, prev, w, bias):
    seqs = prev.shape[0]
    rows, width = x.shape
    taps = w.shape[0]
    gps = rows // seqs // SUBLANES
    x3 = x.reshape(seqs * gps, SUBLANES, width)
    ext = []
    for s in range(seqs):
        ext += [prev[s:s + 1], x3[s * gps:(s + 1) * gps]]
    ext = jnp.concatenate(ext, axis=0)

    def pick(r, first):
        parts = [r[s * (gps + 1) + first:s * (gps + 1) + first + gps] for s in range(seqs)]
        return parts[0] if seqs == 1 else jnp.concatenate(parts, axis=0)

    sub = lax.broadcasted_iota(jnp.int32, x3.shape, 1)
    y = bias + x3 * w[taps - 1:taps]
    for d in range(1, taps):
        r = pltpu.roll(ext, d, 1)
        y = y + jnp.where(sub >= d, pick(r, 1), pick(r, 0)) * w[taps - 1 - d:taps - d]
    return y.reshape(rows, width)


def _last_groups(x, seqs):
    per = x.shape[0] // seqs
    tails = [x[(s + 1) * per - SUBLANES:(s + 1) * per] for s in range(seqs)]
    return jnp.stack(tails, axis=0)


def _lin_scan(a, u, h_in, h_ref):
    tm, width = a.shape
    groups = tm // SUBLANES
    a = a.reshape(groups, SUBLANES, width)
    u = u.reshape(groups, SUBLANES, width)
    sub = lax.broadcasted_iota(jnp.int32, a.shape, 1)
    d = 1
    while d < SUBLANES:
        a_s = pltpu.roll(a, d, 1)
        u_s = pltpu.roll(u, d, 1)
        valid = sub >= d
        u = jnp.where(valid, a * u_s + u, u)
        a = jnp.where(valid, a * a_s, a)
        d *= 2
    carry = h_in
    for g in range(groups):
        h_ref[g * SUBLANES:(g + 1) * SUBLANES] = a[g] * carry + u[g]
        carry = h_ref[(g + 1) * SUBLANES - 1:(g + 1) * SUBLANES]


def _proj_kernel(x_ref, cos_ref, sa_ref, sb_ref, nm_ref, wqkv_ref, wgla_ref, wga_ref, wg2_ref, bg2_ref, wlru_ref,
                 cw_ref, cb_ref, wa_ref, ba_ref, wx_ref, bx_ref, lam_ref, buf0_ref, h0_ref,
                 kt_ref, vr_ref, qb_ref, kb_ref, vt_ref, gq_ref, gk_ref, gv_ref, gr_ref, la_ref, lo_ref,
                 bufo_ref, hl_ref, xbuf, hc, hbuf, *, tm):
    t = pl.program_id(1)
    xn = _rms(x_ref[0], nm_ref[...])
    xb = xn.astype(BF16)

    lx = _dot(xb, wlru_ref[:, :LRU_WIDTH])
    lg = _dot(xb, wlru_ref[:, LRU_WIDTH:])

    @pl.when(t == 0)
    def _():
        xbuf[...] = buf0_ref[...]
        hc[...] = h0_ref[0]

    xc = _causal_conv(lx, xbuf[...], cw_ref[...], cb_ref[...])
    tail = _last_groups(lx, 1)
    bufo_ref[...] = tail
    xbuf[...] = tail

    xcb = xc.astype(BF16)
    r_pre = _dot(xcb, wa_ref[...]) + ba_ref[...]
    i_pre = _dot(xcb, wx_ref[...]) + bx_ref[...]

    c, sa, sb = cos_ref[...], sa_ref[...], sb_ref[...]
    aw = 2 * A_HEADS * A_HEAD_DIM
    q = _rope(_dot(xb, wqkv_ref[:, :aw]), c, sa, sb) * (A_HEAD_DIM ** -0.5 * LOG2E)
    k = _rope(_dot(xb, wqkv_ref[:, aw:2 * aw]), c, sa, sb)
    v = _dot(xb, wqkv_ref[:, 2 * aw:])
    kt_ref[0, 0] = k.T
    for h in range(A_HEADS):
        vr_ref[0, 0, pl.ds(h, tm, stride=A_HEADS), :] = v[:, h * A_VDIM:(h + 1) * A_VDIM]
    kb_ref[0] = k.astype(BF16)
    vt_ref[0, :, 0] = v.T.reshape(A_HEADS, A_VDIM, tm).astype(BF16)
    lane = lax.broadcasted_iota(jnp.int32, (tm, LANES), 1)
    for h in range(A_HEADS):
        seg = q[:, h * LANES:(h + 1) * LANES]
        qb_ref[0, :, 2 * h * LANES:(2 * h + 1) * LANES] = jnp.where(lane < A_HEAD_DIM, seg, 0.0).astype(BF16)
        qb_ref[0, :, (2 * h + 1) * LANES:(2 * h + 2) * LANES] = jnp.where(lane >= A_HEAD_DIM, seg, 0.0).astype(BF16)

    gq_ref[0] = _dot(xb, wgla_ref[:, :G_QK]) * (G_DK ** -0.5)
    gk_ref[0] = _dot(xb, wgla_ref[:, G_QK:2 * G_QK])
    gv_ref[0] = _dot(xb, wgla_ref[:, 2 * G_QK:2 * G_QK + G_WIDTH]).astype(BF16)
    gr_ref[0] = _dot(xb, wgla_ref[:, 2 * G_QK + G_WIDTH:])
    ga = _dot(xb, wga_ref[...])
    gate = _dot(ga.astype(BF16), wg2_ref[...]) + bg2_ref[...]
    la_ref[0] = -_softplus(-gate) * (1.0 / G_GATE_TAU)

    log_a = (-LRU_C) * _sigmoid(r_pre) * _softplus(-lam_ref[...])
    a = jnp.exp(log_a)
    u = _sqrt_bounded(-jnp.tanh(log_a) * (a * a + 1.0)) * (_sigmoid(i_pre) * xc)
    _lin_scan(a, u, hc[...], hbuf)
    h_last = hbuf[tm - 1:tm]
    hc[...] = h_last
    hl_ref[0] = h_last
    lo_ref[0] = (hbuf[...] * _gelu_tanh(lg)).astype(BF16)


def _proj_kernel_stacked(kst_ref, vst_ref, *refs, tm):
    del kst_ref, vst_ref
    _proj_kernel(*refs, tm=tm)


def _proj_call(x, tabs, w, buf0, h0, layer, depth, stacks):
    b, t, d = x.shape
    tm = min(t, 512)
    assert t % tm == 0 and tm % 8 == 0
    nt = t // tm
    row = lambda bi, ti: (bi, ti, 0)
    const2 = lambda bi, ti: (0, 0)
    tab_spec = pl.BlockSpec((tm, LANES), lambda bi, ti: (ti, 0))

    def wspec(a):
        return pl.BlockSpec(a.shape, const2, pipeline_mode=pl.Buffered(1))

    weights = (w['norm_mix'], w['wqkv'], w['wgla'], w['wga'], w['wg2'], w['bg2'], w['wlru'], w['lru_conv_w'],
               w['lru_conv_b'], w['wa_bd'], w['lru_ba'], w['wx_bd'], w['lru_bx'], w['lru_lambda'])
    in_specs = ([pl.BlockSpec((1, tm, d), row), tab_spec, tab_spec, tab_spec] + [wspec(a) for a in weights]
                + [pl.BlockSpec((1, CARRY_ROWS, LRU_WIDTH), lambda bi, ti: (bi, 0, 0)),
                   pl.BlockSpec((1, 1, LRU_WIDTH), lambda bi, ti: (bi, 0, 0))])
    aw = 2 * A_HEADS * A_HEAD_DIM

    def out(wd, dt):
        return jax.ShapeDtypeStruct((b, t, wd), dt), pl.BlockSpec((1, tm, wd), row)

    vt_out = (jax.ShapeDtypeStruct((b, A_HEADS, nt, A_VDIM, tm), BF16),
              pl.BlockSpec((1, A_HEADS, 1, A_VDIM, tm), lambda bi, ti: (bi, 0, ti, 0, 0)))
    kt_out = (jax.ShapeDtypeStruct((depth, b, aw, t), F32),
              pl.BlockSpec((1, 1, aw, tm), lambda bi, ti: (layer, bi, 0, ti)))
    vr_out = (jax.ShapeDtypeStruct((depth, b, t * A_HEADS, A_VDIM), F32),
              pl.BlockSpec((1, 1, tm * A_HEADS, A_VDIM), lambda bi, ti: (layer, bi, ti, 0)))
    outs = [kt_out, vr_out, out(2 * aw, BF16), out(aw, BF16), vt_out,
            out(G_QK, F32), out(G_QK, F32), out(G_WIDTH, BF16), out(G_WIDTH, F32), out(G_QK, F32),
            out(LRU_WIDTH, BF16),
            (jax.ShapeDtypeStruct((b, CARRY_ROWS, LRU_WIDTH), F32),
             pl.BlockSpec((1, CARRY_ROWS, LRU_WIDTH), lambda bi, ti: (bi, 0, 0))),
            (jax.ShapeDtypeStruct((b, 1, LRU_WIDTH), F32), pl.BlockSpec((1, 1, LRU_WIDTH), lambda bi, ti: (bi, 0, 0)))]
    args = (x, *tabs, *weights, buf0, h0)
    body, aliases = _proj_kernel, {}
    if stacks is not None:
        args = (*stacks, *args)
        in_specs = [pl.BlockSpec(memory_space=pl.ANY)] * 2 + in_specs
        body, aliases = _proj_kernel_stacked, {0: 0, 1: 1}
    return pl.pallas_call(
        functools.partial(body, tm=tm),
        grid=(b, nt),
        in_specs=in_specs,
        out_specs=[o[1] for o in outs],
        out_shape=[o[0] for o in outs],
        input_output_aliases=aliases,
        scratch_shapes=[pltpu.VMEM((1, CARRY_ROWS, LRU_WIDTH), F32), pltpu.VMEM((1, LRU_WIDTH), F32),
                        pltpu.VMEM((tm, LRU_WIDTH), F32)],
        compiler_params=_cparams(("arbitrary", "arbitrary")),
        name="proj",
    )(*args)


def _attn_lambda(lq_ref, lam_init):
    lq = lq_ref[...]
    s1 = jnp.sum(lq[0:1] * lq[1:2], axis=1, keepdims=True)
    s2 = jnp.sum(lq[2:3] * lq[3:4], axis=1, keepdims=True)
    return jnp.exp(s1) - jnp.exp(s2) + lam_init


def _attn_norm(o, sub, lam_init):
    ms = jnp.mean(o * o, axis=-1, keepdims=True)
    return (o * lax.rsqrt(ms + EPS) * sub * (1.0 - lam_init)).astype(BF16)


def _chunk_mask(q0, k0, tq, tk, keys_on_rows=False):
    shape = (tk, tq) if keys_on_rows else (tq, tk)
    qpos = q0 + lax.broadcasted_iota(jnp.int32, shape, 1 if keys_on_rows else 0)
    kpos = k0 + lax.broadcasted_iota(jnp.int32, shape, 0 if keys_on_rows else 1)
    return (kpos // CHUNK) <= (qpos // CHUNK)


def _attn_prompt_kernel(lq_ref, sub_ref, q_ref, k_ref, vt_ref, o_ref, m_sc, acc_sc, s_sc, mt_sc,
                        *, tq, tk, tv, lam_init):
    i = pl.program_id(2)
    m_sc[...] = jnp.full(m_sc.shape, NEG_BIG, F32)
    acc_sc[...] = jnp.zeros(acc_sc.shape, F32)
    q = q_ref[0]
    qs = (q[:, :LANES], q[:, LANES:])
    ones = jnp.ones((BF16_SUBLANES, tk), BF16)

    def scores(t, slot, masked, lo=0):
        k = k_ref[0, pl.ds(pl.multiple_of(t * tk, tk), tk), :]
        for c in range(2):
            s = _dot_nt(k, qs[c][lo:])
            if masked:
                s = jnp.where(_chunk_mask(i * tq + lo, t * tk, tq - lo, tk, keys_on_rows=True), s, NEG_BIG)
            s_sc[slot, c, :, lo:] = s
            mt_sc[slot, c, :, lo:] = jnp.max(s, axis=0, keepdims=True)

    def consume(t, slot, lo=0):
        vt = [vt_ref[0, 0, t * (tk // tv) + u] for u in range(tk // tv)]
        vt_aug = jnp.concatenate([_lane_cat(vt), ones], axis=0)
        for c in range(2):
            m_prev = m_sc[c, :, lo:]
            m_new = jnp.maximum(m_prev, mt_sc[slot, c, :, lo:])
            alpha = jnp.exp2(m_prev - m_new)
            p = jnp.exp2(s_sc[slot, c, :, lo:] - m_new).astype(BF16)
            acc_sc[c, :, lo:] = acc_sc[c, :, lo:] * alpha + _dot(vt_aug, p)
            m_sc[c, :, lo:] = m_new

    n_diag = tq // tk
    n_full = i * n_diag
    n_plain = jnp.maximum(n_full - 1, 0)
    scores(0, 0, True)

    def pair(p, carry):
        t = 2 * p
        scores(t + 1, 1, False)
        consume(t, 0)
        scores(t + 2, 0, False)
        consume(t + 1, 1)
        return carry

    lax.fori_loop(0, n_plain // 2, pair, 0)
    tc = (n_plain // 2) * 2

    def tail(n_unmasked, n_masked):
        count = n_unmasked + n_masked
        lo = [max(u - n_unmasked, 0) * tk for u in range(count)]
        for u in range(count - 1):
            scores(tc + u + 1, (u + 1) % 2, u + 1 >= n_unmasked, lo[u + 1])
            consume(tc + u, u % 2, lo[u])
        consume(tc + count - 1, (count - 1) % 2, lo[count - 1])

    @pl.when(n_full == 0)
    def _():
        tail(0, n_diag)

    @pl.when(jnp.logical_and(n_full > 0, n_plain % 2 == 0))
    def _():
        tail(1, n_diag)

    @pl.when(jnp.logical_and(n_full > 0, n_plain % 2 == 1))
    def _():
        tail(2, n_diag)

    a0, a1 = acc_sc[0], acc_sc[1]
    lam = _attn_lambda(lq_ref, lam_init)
    o_t = a0[:A_VDIM] / a0[A_VDIM:A_VDIM + 1] - lam * (a1[:A_VDIM] / a1[A_VDIM:A_VDIM + 1])
    o_ref[0] = _attn_norm(o_t.T, sub_ref[...], lam_init)


def _lane_cat(parts):
    return parts[0] if len(parts) == 1 else jnp.concatenate(parts, axis=1)


def _attn_prompt_call(lq, sub, qb, kb, vt, lam_init):
    b, t, _ = kb.shape
    tv = vt.shape[-1]
    tq = min(t, 2048)
    tk = min(t, 512)
    assert t % tq == 0 and (tq % CHUNK == 0 or tq == t) and tq % LANES == 0 and tk % tv == 0 and tq % tk == 0
    return pl.pallas_call(
        functools.partial(_attn_prompt_kernel, tq=tq, tk=tk, tv=tv, lam_init=lam_init),
        grid=(b, A_HEADS, t // tq),
        in_specs=[pl.BlockSpec(lq.shape, lambda bi, h, i: (0, 0)),
                  pl.BlockSpec(sub.shape, lambda bi, h, i: (0, 0)),
                  pl.BlockSpec((1, tq, 2 * LANES), lambda bi, h, i: (bi, i, h)),
                  pl.BlockSpec((1, t, LANES), lambda bi, h, i: (bi, 0, h)),
                  pl.BlockSpec((1, 1, t // tv, A_VDIM, tv), lambda bi, h, i: (bi, h, 0, 0, 0))],
        out_specs=pl.BlockSpec((1, tq, LANES), lambda bi, h, i: (bi, i, h)),
        out_shape=jax.ShapeDtypeStruct((b, t, A_WIDTH), BF16),
        scratch_shapes=[pltpu.VMEM((2, 1, tq), F32), pltpu.VMEM((2, A_VDIM + BF16_SUBLANES, tq), F32),
                        pltpu.VMEM((2, 2, tk, tq), F32), pltpu.VMEM((2, 2, 1, tq), F32)],
        compiler_params=_cparams(("arbitrary", "arbitrary", "arbitrary")),
        name="attn_prompt",
    )(lq, sub, qb, kb, vt)


def _attn_cached_kernel(lq_ref, sub_ref, q_ref, kpt_ref, vp_ref, kn_ref, vn_ref, o_ref, *, t, p_len, lam_init):
    mask_p = jnp.concatenate([_chunk_mask(p_len, 0, t, p_len)] * 2, axis=0)
    mask_n = jnp.concatenate([_chunk_mask(p_len, p_len, t, t)] * 2, axis=0)
    lam = _attn_lambda(lq_ref, lam_init)
    ones_p = jnp.ones((p_len, LANES), BF16)
    ones_n = jnp.ones((t, LANES), BF16)
    for h in range(A_HEADS):
        cols = slice(h * LANES, (h + 1) * LANES)
        kpt = kpt_ref[0, 0, cols, :].astype(BF16)
        kn = kn_ref[0, :, cols]
        vp = jnp.concatenate([vp_ref[0, 0, pl.ds(h, p_len, stride=A_HEADS), :].astype(BF16), ones_p], axis=1)
        vn = jnp.concatenate([vn_ref[0, 0, pl.ds(h, t, stride=A_HEADS), :].astype(BF16), ones_n], axis=1)
        q2 = jnp.concatenate([q_ref[0, :, 2 * h * LANES:(2 * h + 1) * LANES],
                              q_ref[0, :, (2 * h + 1) * LANES:(2 * h + 2) * LANES]], axis=0)
        s_p = jnp.where(mask_p, _dot(q2, kpt), NEG_BIG)
        s_n = jnp.where(mask_n, _dot_nt(q2, kn), NEG_BIG)
        m = jnp.maximum(jnp.max(s_p, axis=1, keepdims=True), jnp.max(s_n, axis=1, keepdims=True))
        acc = _dot(jnp.exp2(s_p - m).astype(BF16), vp) + _dot(jnp.exp2(s_n - m).astype(BF16), vn)
        ratio = acc[:, :LANES] / acc[:, LANES:]
        o = ratio[:t] - lam * ratio[t:]
        o_ref[0, :, cols] = _attn_norm(o, sub_ref[...], lam_init)


def _attn_cached_call(lq, sub, qb, k_new, v_new_rows, k_past_t, v_past_rows, layer, lam_init):
    b, t, _ = k_new.shape
    p_len = k_past_t.shape[-1]
    aw = 2 * A_HEADS * A_HEAD_DIM
    return pl.pallas_call(
        functools.partial(_attn_cached_kernel, t=t, p_len=p_len, lam_init=lam_init),
        grid=(b,),
        in_specs=[pl.BlockSpec(lq.shape, lambda bi: (0, 0)),
                  pl.BlockSpec(sub.shape, lambda bi: (0, 0)),
                  pl.BlockSpec((1, t, 2 * aw), lambda bi: (bi, 0, 0)),
                  pl.BlockSpec((1, 1, aw, p_len), lambda bi: (layer, bi, 0, 0)),
                  pl.BlockSpec((1, 1, p_len * A_HEADS, A_VDIM), lambda bi: (layer, bi, 0, 0)),
                  pl.BlockSpec((1, t, aw), lambda bi: (bi, 0, 0)),
                  pl.BlockSpec((1, 1, t * A_HEADS, A_VDIM), lambda bi: (layer, bi, 0, 0))],
        out_specs=pl.BlockSpec((1, t, A_WIDTH), lambda bi: (bi, 0, 0)),
        out_shape=jax.ShapeDtypeStruct((b, t, A_WIDTH), BF16),
        compiler_params=_cparams(("arbitrary",)),
        name="attn_cached",
    )(lq, sub, qb, k_past_t, v_past_rows, k_new, v_new_rows)


def _split3(x):
    hi = x.astype(BF16)
    r1 = x - hi.astype(F32)
    mid = r1.astype(BF16)
    lo = (r1 - mid.astype(F32)).astype(BF16)
    return hi, mid, lo


def _gla_kernel(q_ref, k_ref, v_ref, r_ref, la_ref, gn_ref, st0_ref, o_ref, sto_ref, st, *, tg, lc):
    t = pl.program_id(1)

    @pl.when(t == 0)
    def _():
        st[...] = st0_ref[0]

    rr = lax.broadcasted_iota(jnp.int32, (lc, lc), 0)
    cc = lax.broadcasted_iota(jnp.int32, (lc, lc), 1)
    tri = rr >= cc
    tri_b = jnp.where(tri, 1.0, 0.0).astype(BF16)
    lane_k = lax.broadcasted_iota(jnp.int32, (lc, G_QK), 1) // G_DK
    bd = (lax.broadcasted_iota(jnp.int32, (G_WIDTH, G_QK), 0) // G_DV
          == lax.broadcasted_iota(jnp.int32, (G_WIDTH, G_QK), 1) // G_DK)
    gn = gn_ref[...]

    per_chunk = []
    for ci in range(tg // lc):
        rows = slice(ci * lc, (ci + 1) * lc)
        hi, mid, lo = _split3(la_ref[0, rows, :])
        bc = _dot(tri_b, hi) + _dot(tri_b, mid) + _dot(tri_b, lo)
        b_mid = bc[lc // 2 - 1:lc // 2]
        b_last = bc[lc - 1:lc]
        q = q_ref[0, rows, :]
        k = k_ref[0, rows, :]
        vb = v_ref[0, rows, :]
        q_in = (q * jnp.exp(bc)).astype(BF16)
        q_mid = q * jnp.exp(bc - b_mid)
        k_mid = (k * jnp.exp(b_mid - bc)).astype(BF16)
        k_end = (k * jnp.exp(b_last - bc)).astype(BF16)
        att = []
        for h in range(G_HEADS):
            qh = jnp.where(lane_k == h, q_mid, 0.0).astype(BF16)
            att.append(jnp.where(tri, _dot_nt(qh, k_mid), 0.0).astype(BF16))
        incr = jnp.where(bd, _dot_tn(vb, k_end), 0.0)
        per_chunk.append((rows, q_in, att, vb, incr, jnp.exp(b_last)))

    intras = [[_dot(att[h], vb[:, h * G_DV:(h + 1) * G_DV]) for h in range(G_HEADS)]
              for _, _, att, vb, _, _ in per_chunk]

    s_cur = st[...]
    for (rows, q_in, _, _, incr, decay), intra in zip(per_chunk, intras):
        inter = _dot_nt(q_in, s_cur.astype(BF16))
        outs = []
        for h in range(G_HEADS):
            oh = inter[:, h * G_DV:(h + 1) * G_DV] + intra[h]
            ms = jnp.mean(oh * oh, axis=-1, keepdims=True)
            outs.append(oh * lax.rsqrt(ms + EPS) * gn)
        gr = r_ref[0, rows, :]
        o_ref[0, rows, :] = (jnp.concatenate(outs, axis=1) * (gr * _sigmoid(gr))).astype(BF16)
        s_cur = s_cur * decay + incr
    st[...] = s_cur
    sto_ref[0] = s_cur


def _gla_call(gq, gk, gv, gr, la, gn, st0):
    b, t, _ = gq.shape
    lc = min(CHUNK, t)
    tg = min(t, 512)
    assert t % tg == 0 and tg % lc == 0 and lc % 16 == 0
    row = lambda bi, ti: (bi, ti, 0)
    st_spec = pl.BlockSpec((1, G_WIDTH, G_QK), lambda bi, ti: (bi, 0, 0))
    return pl.pallas_call(
        functools.partial(_gla_kernel, tg=tg, lc=lc),
        grid=(b, t // tg),
        in_specs=[pl.BlockSpec((1, tg, G_QK), row), pl.BlockSpec((1, tg, G_QK), row),
                  pl.BlockSpec((1, tg, G_WIDTH), row), pl.BlockSpec((1, tg, G_WIDTH), row),
                  pl.BlockSpec((1, tg, G_QK), row), pl.BlockSpec(gn.shape, lambda bi, ti: (0, 0)), st_spec],
        out_specs=[pl.BlockSpec((1, tg, G_WIDTH), row), st_spec],
        out_shape=[jax.ShapeDtypeStruct((b, t, G_WIDTH), BF16), jax.ShapeDtypeStruct((b, G_WIDTH, G_QK), F32)],
        scratch_shapes=[pltpu.VMEM((G_WIDTH, G_QK), F32)],
        compiler_params=_cparams(("arbitrary", "arbitrary")),
        name="gla",
    )(gq, gk, gv, gr, la, gn, st0)


def _merge_kernel(x_ref, ao_ref, go_ref, lo_ref, nm_ref, wm_ref, bm_ref, wba_ref, wbg_ref, wbl_ref, wo_ref, o_ref):
    x = x_ref[...]
    xb = _rms(x, nm_ref[...]).astype(BF16)
    g = _sigmoid(_dot(xb, wm_ref[...]) + bm_ref[...])
    merged = (g[:, :D_MODEL] * _dot(ao_ref[...], wba_ref[...])
              + g[:, D_MODEL:2 * D_MODEL] * _dot(go_ref[...], wbg_ref[...])
              + g[:, 2 * D_MODEL:] * _dot(lo_ref[...], wbl_ref[...]))
    o_ref[...] = x + _dot(merged.astype(BF16), wo_ref[...])


def _merge_call(x2, ao, go, lo, w):
    n, d = x2.shape
    tm = min(n, 512)
    assert n % tm == 0
    row = lambda i: (i, 0)
    weights = (w['norm_mix'], w['w_merge'], w['b_merge'], w['w_branch_attn'], w['w_branch_gla'], w['w_branch_lru'],
               w['w_out'])
    wspecs = [pl.BlockSpec(a.shape, lambda i: (0, 0), pipeline_mode=pl.Buffered(1)) for a in weights]
    return pl.pallas_call(
        _merge_kernel,
        grid=(n // tm,),
        in_specs=[pl.BlockSpec((tm, d), row), pl.BlockSpec((tm, A_WIDTH), row), pl.BlockSpec((tm, G_WIDTH), row),
                  pl.BlockSpec((tm, LRU_WIDTH), row)] + wspecs,
        out_specs=pl.BlockSpec((tm, d), row),
        out_shape=jax.ShapeDtypeStruct((n, d), F32),
        compiler_params=_cparams(("arbitrary",)),
        name="merge",
    )(x2, ao, go, lo, *weights)


def _ffn_kernel(x_ref, nf_ref, wg_ref, cw_ref, cb_ref, wu_ref, wd_ref, nl_ref, buf0_ref, o_ref, bufo_ref, ubuf,
                *, seqs, tm, final):
    t = pl.program_id(1)
    x = x_ref[...].reshape(seqs * tm, D_MODEL)
    hb = _rms(x, nf_ref[...]).astype(BF16)

    @pl.when(t == 0)
    def _():
        ubuf[...] = buf0_ref[...]

    gu = _dot(hb, wg_ref[...])
    gc = _causal_conv(gu, ubuf[...], cw_ref[...], cb_ref[...])
    tail = _last_groups(gu, seqs)
    bufo_ref[...] = tail
    ubuf[...] = tail
    f = _gelu_tanh(gc) * _dot(hb, wu_ref[...])
    y = x + _dot(f.astype(BF16), wd_ref[...])
    y = _rms(y, nl_ref[...]) if final else y
    o_ref[...] = y.reshape(seqs, tm, D_MODEL)


def _ffn_call(x, w, norm_last, buf0, final):
    b, t, d = x.shape
    tm = min(t, 512)
    assert t % tm == 0
    seqs = max(1, min(b, 256 // tm)) if tm == t else 1
    assert b % seqs == 0
    row = lambda bi, ti: (bi, ti, 0)
    const2 = lambda bi, ti: (0, 0)
    buf_spec = pl.BlockSpec((seqs, CARRY_ROWS, D_FF), lambda bi, ti: (bi, 0, 0))
    weights = (w['norm_ffn'], w['w_ffn_gate'], w['ffn_conv_w'], w['ffn_conv_b'], w['w_ffn_up'], w['w_ffn_down'],
               norm_last)
    wspecs = [pl.BlockSpec(a.shape, const2, pipeline_mode=pl.Buffered(1)) for a in weights]
    return pl.pallas_call(
        functools.partial(_ffn_kernel, seqs=seqs, tm=tm, final=final),
        grid=(b // seqs, t // tm),
        in_specs=[pl.BlockSpec((seqs, tm, d), row)] + wspecs + [buf_spec],
        out_specs=[pl.BlockSpec((seqs, tm, d), row), buf_spec],
        out_shape=[jax.ShapeDtypeStruct((b, t, d), F32), jax.ShapeDtypeStruct((b, CARRY_ROWS, D_FF), F32)],
        scratch_shapes=[pltpu.VMEM((seqs, CARRY_ROWS, D_FF), F32)],
        compiler_params=_cparams(("arbitrary", "arbitrary")),
        name="ffn",
    )(x, *weights, buf0)


def _rope_tables(p_len, t):
    half = ROPE_DIM // 2
    pos = (p_len + jnp.arange(t, dtype=jnp.int32)).astype(F32)
    inv = ROPE_THETA ** (-jnp.arange(half, dtype=F32) / half)
    ang = pos[:, None] * inv[None, :]
    cos, sin = jnp.cos(ang), jnp.sin(ang)
    rest = A_HEAD_DIM - ROPE_DIM
    z = jnp.zeros((t, half), F32)
    c64 = jnp.concatenate([cos, cos, jnp.ones((t, rest), F32)], axis=1)
    sa64 = jnp.concatenate([-sin, z, jnp.zeros((t, rest), F32)], axis=1)
    sb64 = jnp.concatenate([z, sin, jnp.zeros((t, rest), F32)], axis=1)
    rep = LANES // A_HEAD_DIM
    return tuple(jnp.tile(a, (1, rep)) for a in (c64, sa64, sb64))


def _block_diag(wb):
    n, bi, bo = wb.shape
    eye = jnp.eye(n, dtype=wb.dtype)
    return (eye[:, None, :, None] * wb[:, :, None, :]).reshape(n * bi, n * bo)


def _prep_layer(l, p):
    aw = 2 * A_HEADS * A_HEAD_DIM
    o_gla = 2 * aw + A_WIDTH
    o_ga = o_gla + 2 * G_QK + 2 * G_WIDTH
    o_lru = o_ga + G_GATE_RANK
    w_in = p['w_in'][l]
    row = lambda a: a[l].reshape(1, -1)
    return {
        'norm_mix': row(p['norm_mix']),
        'wqkv': w_in[:, :o_gla].astype(BF16),
        'wgla': w_in[:, o_gla:o_ga].astype(BF16),
        'wga': w_in[:, o_ga:o_lru].astype(BF16),
        'wlru': w_in[:, o_lru:].astype(BF16),
        'wg2': p['w_gla_gate2'][l].astype(BF16),
        'bg2': row(p['b_gla_gate']),
        'lambda_qk': p['lambda_qk'][l],
        'attn_subln': row(p['attn_subln']),
        'gla_norm': row(p['gla_norm']),
        'lru_conv_w': p['lru_conv_w'][l],
        'lru_conv_b': row(p['lru_conv_b']),
        'wa_bd': _block_diag(p['lru_wa'][l]).astype(BF16),
        'lru_ba': row(p['lru_ba']),
        'wx_bd': _block_diag(p['lru_wx'][l]).astype(BF16),
        'lru_bx': row(p['lru_bx']),
        'lru_lambda': row(p['lru_lambda']),
        'w_branch_attn': p['w_branch_attn'][l].astype(BF16),
        'w_branch_gla': p['w_branch_gla'][l].astype(BF16),
        'w_branch_lru': p['w_branch_lru'][l].astype(BF16),
        'w_merge': p['w_merge'][l].astype(BF16),
        'b_merge': row(p['b_merge']),
        'w_out': p['w_out'][l].astype(BF16),
        'norm_ffn': row(p['norm_ffn']),
        'w_ffn_gate': p['w_ffn_gate'][l].astype(BF16),
        'ffn_conv_w': p['ffn_conv_w'][l],
        'ffn_conv_b': row(p['ffn_conv_b']),
        'w_ffn_up': p['w_ffn_up'][l].astype(BF16),
        'w_ffn_down': p['w_ffn_down'][l].astype(BF16),
    }


def _pad_carry(buf):
    return jnp.pad(buf, ((0, 0), (CARRY_ROWS - buf.shape[1], 0), (0, 0)))


def _state_to_kernel(s):
    b = s.shape[0]
    eye = jnp.eye(G_HEADS, dtype=s.dtype)
    st = jnp.swapaxes(s, 2, 3)
    return (st[:, :, :, None, :] * eye[None, :, None, :, None]).reshape(b, G_WIDTH, G_QK)


def _state_from_kernel(st):
    b = st.shape[0]
    s5 = st.reshape(b, G_HEADS, G_DV, G_HEADS, G_DK)
    diag = jnp.stack([s5[:, h, :, h, :] for h in range(G_HEADS)], axis=1)
    return jnp.swapaxes(diag, 2, 3)


def _trunk(x, caches, layers, norm_final):
    cache_k, cache_v, st_gla, st_lconv, st_lh, st_fconv = caches
    b, t, d = x.shape
    p_len = 0 if cache_k is None else cache_k.shape[2]
    tabs = _rope_tables(p_len, t)
    if cache_k is not None:
        depth = cache_k.shape[0]
        cache_kt = jnp.transpose(cache_k, (0, 1, 3, 4, 5, 2)).reshape(depth, b, -1, p_len)
        cache_vr = cache_v.reshape(depth, b, p_len * A_HEADS, A_VDIM)
    outs = [[], [], [], [], [], []]
    stacks = None
    for l, w in enumerate(layers):
        lam_init = 0.8 - 0.6 * math.exp(-0.3 * l)
        (kt_all, vr_all, qb, kb, vt, gq, gk, gv, gr, la, lo, lbuf, hl) = _proj_call(
            x, tabs, w, _pad_carry(st_lconv[l]), st_lh[l].reshape(b, 1, LRU_WIDTH), l, len(layers), stacks)
        stacks = (kt_all, vr_all)
        if cache_k is None:
            ao = _attn_prompt_call(w['lambda_qk'], w['attn_subln'], qb, kb, vt, lam_init)
        else:
            ao = _attn_cached_call(w['lambda_qk'], w['attn_subln'], qb, kb, vr_all, cache_kt, cache_vr, l, lam_init)
        go, st_new = _gla_call(gq, gk, gv, gr, la, w['gla_norm'], _state_to_kernel(st_gla[l]))
        x1 = _merge_call(x.reshape(b * t, d), ao.reshape(b * t, -1), go.reshape(b * t, -1), lo.reshape(b * t, -1), w)
        x, fbuf = _ffn_call(x1.reshape(b, t, d), w, norm_final.reshape(1, -1), _pad_carry(st_fconv[l]),
                            final=(l == len(layers) - 1))
        outs[2].append(_state_from_kernel(st_new))
        outs[3].append(lbuf[:, CARRY_ROWS - (LRU_CONV - 1):])
        outs[4].append(hl.reshape(b, LRU_WIDTH))
        outs[5].append(fbuf[:, CARRY_ROWS - (FFN_CONV - 1):])
    kt_all, vr_all = stacks
    depth = len(layers)
    new_k = jnp.transpose(kt_all.reshape(depth, b, A_HEADS, 2, A_HEAD_DIM, t), (0, 1, 5, 2, 3, 4))
    new_v = vr_all.reshape(depth, b, t, A_HEADS, A_VDIM)
    return x, [new_k, new_v] + [jnp.stack(o) for o in outs[2:]]


@jax.jit
def _forward(x_prompt, x_sample, cache_attn_k, cache_attn_v, state_gla, state_lru_conv, state_lru_h, state_ffn_conv,
             params, norm_final):
    layers = [_prep_layer(l, params) for l in range(DEPTH)]
    bp = x_prompt.shape[0]
    zeros = (None, None,
             jnp.zeros((DEPTH, bp, G_HEADS, G_DK, G_DV), F32),
             jnp.zeros((DEPTH, bp, LRU_CONV - 1, LRU_WIDTH), F32),
             jnp.zeros((DEPTH, bp, LRU_WIDTH), F32),
             jnp.zeros((DEPTH, bp, FFN_CONV - 1, D_FF), F32))
    y_p, new_p = _trunk(x_prompt, zeros, layers, norm_final)
    y_s, new_s = _trunk(x_sample, (cache_attn_k, cache_attn_v, state_gla, state_lru_conv, state_lru_h,
                                   state_ffn_conv), layers, norm_final)
    return (y_p, y_s, *new_p, *new_s)


def kernel(x_prompt, x_sample, cache_attn_k, cache_attn_v, state_gla, state_lru_conv, state_lru_h, state_ffn_conv, norm_mix, w_in, lambda_qk, attn_subln, w_gla_gate2, b_gla_gate, gla_norm, lru_conv_w, lru_conv_b, lru_wa, lru_ba, lru_wx, lru_bx, lru_lambda, w_branch_attn, w_branch_gla, w_branch_lru, w_merge, b_merge, w_out, norm_ffn, w_ffn_gate, ffn_conv_w, ffn_conv_b, w_ffn_up, w_ffn_down, norm_final):
    params = dict(norm_mix=norm_mix, w_in=w_in, lambda_qk=lambda_qk, attn_subln=attn_subln, w_gla_gate2=w_gla_gate2,
                  b_gla_gate=b_gla_gate, gla_norm=gla_norm, lru_conv_w=lru_conv_w, lru_conv_b=lru_conv_b,
                  lru_wa=lru_wa, lru_ba=lru_ba, lru_wx=lru_wx, lru_bx=lru_bx, lru_lambda=lru_lambda,
                  w_branch_attn=w_branch_attn, w_branch_gla=w_branch_gla, w_branch_lru=w_branch_lru,
                  w_merge=w_merge, b_merge=b_merge, w_out=w_out, norm_ffn=norm_ffn, w_ffn_gate=w_ffn_gate,
                  ffn_conv_w=ffn_conv_w, ffn_conv_b=ffn_conv_b, w_ffn_up=w_ffn_up, w_ffn_down=w_ffn_down)
    return _forward(x_prompt, x_sample, cache_attn_k, cache_attn_v, state_gla, state_lru_conv, state_lru_h,
                    state_ffn_conv, params, norm_final)
```

```python
import functools
import math

import jax
import jax.numpy as jnp
from jax import lax
from jax.experimental import pallas as pl
from jax.experimental.pallas import tpu as pltpu

F32 = jnp.float32
BF16 = jnp.bfloat16

D_MODEL = 1024
DEPTH = 4
CHUNK = 64
EPS = 1e-6
A_HEADS = 4
A_HEAD_DIM = 64
A_VDIM = 128
A_WIDTH = A_HEADS * A_VDIM
ROPE_DIM = 16
ROPE_THETA = 500000.0
G_HEADS = 4
G_DK = 64
G_DV = 128
G_QK = G_HEADS * G_DK
G_WIDTH = G_HEADS * G_DV
G_GATE_RANK = 16
G_GATE_TAU = 16.0
LRU_WIDTH = 512
LRU_BLOCKS = 8
LRU_CONV = 4
LRU_C = 8.0
D_FF = 2816
FFN_CONV = 3

LANES = 128
SUBLANES = 8
BF16_SUBLANES = 16
LOG2E = math.log2(math.e)
CARRY_ROWS = 8
NEG_BIG = -1e30
VMEM_LIMIT = 56 * 1024 * 1024


def _cparams(sem):
    return pltpu.CompilerParams(dimension_semantics=sem, vmem_limit_bytes=VMEM_LIMIT)


def _rms(x, g):
    return x * lax.rsqrt(jnp.mean(x * x, axis=-1, keepdims=True) + EPS) * g


def _sigmoid(x):
    return 0.5 * jnp.tanh(0.5 * x) + 0.5


def _sqrt_bounded(x):
    return jnp.where(x == 0.0, 0.0, x * lax.rsqrt(x))


def _softplus(x):
    return jnp.maximum(x, 0.0) + jnp.log1p(jnp.exp(-jnp.abs(x)))


def _gelu_tanh(x):
    return x * (0.5 * (1.0 + jnp.tanh(math.sqrt(2.0 / math.pi) * (x + 0.044715 * (x * x * x)))))


def _dot(a, b):
    return jnp.dot(a, b, preferred_element_type=F32)


def _dot_nt(a, b):
    return lax.dot_general(a, b, (((1,), (1,)), ((), ())), preferred_element_type=F32)


def _dot_tn(a, b):
    return lax.dot_general(a, b, (((0,), (0,)), ((), ())), preferred_element_type=F32)


def _lane_tile(x, n):
    return x if n == 1 else jnp.concatenate([x] * n, axis=1)


def _rope(x, c, sa, sb):
    segs = []
    for g in range(x.shape[1] // LANES):
        seg = x[:, g * LANES:(g + 1) * LANES]
        seg_up = pltpu.roll(seg, LANES - ROPE_DIM // 2, 1)
        seg_dn = pltpu.roll(seg, ROPE_DIM // 2, 1)
        segs.append(seg * c + seg_up * sa + seg_dn * sb)
    return jnp.concatenate(segs, axis=1)


def _causal_conv(x, prev, w, bias):
    seqs = prev.shape[0]
    rows, width = x.shape
    taps = w.shape[0]
    gps = rows // seqs // SUBLANES
    x3 = x.reshape(seqs * gps, SUBLANES, width)
    ext = []
    for s in range(seqs):
        ext += [prev[s:s + 1], x3[s * gps:(s + 1) * gps]]
    ext = jnp.concatenate(ext, axis=0)

    def pick(r, first):
        parts = [r[s * (gps + 1) + first:s * (gps + 1) + first + gps] for s in range(seqs)]
        return parts[0] if seqs == 1 else jnp.concatenate(parts, axis=0)

    sub = lax.broadcasted_iota(jnp.int32, x3.shape, 1)
    y = bias + x3 * w[taps - 1:taps]
    for d in range(1, taps):
        r = pltpu.roll(ext, d, 1)
        y = y + jnp.where(sub >= d, pick(r, 1), pick(r, 0)) * w[taps - 1 - d:taps - d]
    return y.reshape(rows, width)


def _last_groups(x, seqs):
    per = x.shape[0] // seqs
    tails = [x[(s + 1) * per - SUBLANES:(s + 1) * per] for s in range(seqs)]
    return jnp.stack(tails, axis=0)


def _lin_scan(a, u, h_in, h_ref):
    tm, width = a.shape
    groups = tm // SUBLANES
    a = a.reshape(groups, SUBLANES, width)
    u = u.reshape(groups, SUBLANES, width)
    sub = lax.broadcasted_iota(jnp.int32, a.shape, 1)
    d = 1
    while d < SUBLANES:
        a_s = pltpu.roll(a, d, 1)
        u_s = pltpu.roll(u, d, 1)
        valid = sub >= d
        u = jnp.where(valid, a * u_s + u, u)
        a = jnp.where(valid, a * a_s, a)
        d *= 2
    carry = h_in
    for g in range(groups):
        h_ref[g * SUBLANES:(g + 1) * SUBLANES] = a[g] * carry + u[g]
        carry = h_ref[(g + 1) * SUBLANES - 1:(g + 1) * SUBLANES]


def _proj_kernel(x_ref, cos_ref, sa_ref, sb_ref, nm_ref, wqkv_ref, wgla_ref, wga_ref, wg2_ref, bg2_ref, wlru_ref,
                 cw_ref, cb_ref, wa_ref, ba_ref, wx_ref, bx_ref, lam_ref, buf0_ref, h0_ref,
                 kt_ref, vr_ref, qb_ref, kb_ref, vt_ref, gq_ref, gk_ref, gv_ref, gr_ref, la_ref, lo_ref,
                 bufo_ref, hl_ref, xbuf, hc, hbuf, *, tm):
    t = pl.program_id(1)
    xn = _rms(x_ref[0], nm_ref[...])
    xb = xn.astype(BF16)

    lx = _dot(xb, wlru_ref[:, :LRU_WIDTH])
    lg = _dot(xb, wlru_ref[:, LRU_WIDTH:])

    @pl.when(t == 0)
    def _():
        xbuf[...] = buf0_ref[...]
        hc[...] = h0_ref[0]

    xc = _causal_conv(lx, xbuf[...], cw_ref[...], cb_ref[...])
    tail = _last_groups(lx, 1)
    bufo_ref[...] = tail
    xbuf[...] = tail

    xcb = xc.astype(BF16)
    r_pre = _dot(xcb, wa_ref[...]) + ba_ref[...]
    i_pre = _dot(xcb, wx_ref[...]) + bx_ref[...]

    c, sa, sb = cos_ref[...], sa_ref[...], sb_ref[...]
    aw = 2 * A_HEADS * A_HEAD_DIM
    q = _rope(_dot(xb, wqkv_ref[:, :aw]), c, sa, sb) * (A_HEAD_DIM ** -0.5 * LOG2E)
    k = _rope(_dot(xb, wqkv_ref[:, aw:2 * aw]), c, sa, sb)
    v = _dot(xb, wqkv_ref[:, 2 * aw:])
    kt_ref[0, 0] = k.T
    for h in range(A_HEADS):
        vr_ref[0, 0, pl.ds(h, tm, stride=A_HEADS), :] = v[:, h * A_VDIM:(h + 1) * A_VDIM]
    kb_ref[0] = k.astype(BF16)
    vt_ref[0, :, 0] = v.T.reshape(A_HEADS, A_VDIM, tm).astype(BF16)
    lane = lax.broadcasted_iota(jnp.int32, (tm, LANES), 1)
    for h in range(A_HEADS):
        seg = q[:, h * LANES:(h + 1) * LANES]
        qb_ref[0, :, 2 * h * LANES:(2 * h + 1) * LANES] = jnp.where(lane < A_HEAD_DIM, seg, 0.0).astype(BF16)
        qb_ref[0, :, (2 * h + 1) * LANES:(2 * h + 2) * LANES] = jnp.where(lane >= A_HEAD_DIM, seg, 0.0).astype(BF16)

    gq_ref[0] = _dot(xb, wgla_ref[:, :G_QK]) * (G_DK ** -0.5)
    gk_ref[0] = _dot(xb, wgla_ref[:, G_QK:2 * G_QK])
    gv_ref[0] = _dot(xb, wgla_ref[:, 2 * G_QK:2 * G_QK + G_WIDTH]).astype(BF16)
    gr_ref[0] = _dot(xb, wgla_ref[:, 2 * G_QK + G_WIDTH:])
    ga = _dot(xb, wga_ref[...])
    gate = _dot(ga.astype(BF16), wg2_ref[...]) + bg2_ref[...]
    la_ref[0] = -_softplus(-gate) * (1.0 / G_GATE_TAU)

    log_a = (-LRU_C) * _sigmoid(r_pre) * _softplus(-lam_ref[...])
    a = jnp.exp(log_a)
    u = _sqrt_bounded(-jnp.tanh(log_a) * (a * a + 1.0)) * (_sigmoid(i_pre) * xc)
    _lin_scan(a, u, hc[...], hbuf)
    h_last = hbuf[tm - 1:tm]
    hc[...] = h_last
    hl_ref[0] = h_last
    lo_ref[0] = (hbuf[...] * _gelu_tanh(lg)).astype(BF16)


def _proj_kernel_stacked(kst_ref, vst_ref, *refs, tm):
    del kst_ref, vst_ref
    _proj_kernel(*refs, tm=tm)


def _proj_call(x, tabs, w, buf0, h0, layer, depth, stacks):
    b, t, d = x.shape
    tm = min(t, 512)
    assert t % tm == 0 and tm % 8 == 0
    nt = t // tm
    row = lambda bi, ti: (bi, ti, 0)
    const2 = lambda bi, ti: (0, 0)
    tab_spec = pl.BlockSpec((tm, LANES), lambda bi, ti: (ti, 0))

    def wspec(a):
        return pl.BlockSpec(a.shape, const2, pipeline_mode=pl.Buffered(1))

    weights = (w['norm_mix'], w['wqkv'], w['wgla'], w['wga'], w['wg2'], w['bg2'], w['wlru'], w['lru_conv_w'],
               w['lru_conv_b'], w['wa_bd'], w['lru_ba'], w['wx_bd'], w['lru_bx'], w['lru_lambda'])
    in_specs = ([pl.BlockSpec((1, tm, d), row), tab_spec, tab_spec, tab_spec] + [wspec(a) for a in weights]
                + [pl.BlockSpec((1, CARRY_ROWS, LRU_WIDTH), lambda bi, ti: (bi, 0, 0)),
                   pl.BlockSpec((1, 1, LRU_WIDTH), lambda bi, ti: (bi, 0, 0))])
    aw = 2 * A_HEADS * A_HEAD_DIM

    def out(wd, dt):
        return jax.ShapeDtypeStruct((b, t, wd), dt), pl.BlockSpec((1, tm, wd), row)

    vt_out = (jax.ShapeDtypeStruct((b, A_HEADS, nt, A_VDIM, tm), BF16),
              pl.BlockSpec((1, A_HEADS, 1, A_VDIM, tm), lambda bi, ti: (bi, 0, ti, 0, 0)))
    kt_out = (jax.ShapeDtypeStruct((depth, b, aw, t), F32),
              pl.BlockSpec((1, 1, aw, tm), lambda bi, ti: (layer, bi, 0, ti)))
    vr_out = (jax.ShapeDtypeStruct((depth, b, t * A_HEADS, A_VDIM), F32),
              pl.BlockSpec((1, 1, tm * A_HEADS, A_VDIM), lambda bi, ti: (layer, bi, ti, 0)))
    outs = [kt_out, vr_out, out(2 * aw, BF16), out(aw, BF16), vt_out,
            out(G_QK, F32), out(G_QK, F32), out(G_WIDTH, BF16), out(G_WIDTH, F32), out(G_QK, F32),
            out(LRU_WIDTH, BF16),
            (jax.ShapeDtypeStruct((b, CARRY_ROWS, LRU_WIDTH), F32),
             pl.BlockSpec((1, CARRY_ROWS, LRU_WIDTH), lambda bi, ti: (bi, 0, 0))),
            (jax.ShapeDtypeStruct((b, 1, LRU_WIDTH), F32), pl.BlockSpec((1, 1, LRU_WIDTH), lambda bi, ti: (bi, 0, 0)))]
    args = (x, *tabs, *weights, buf0, h0)
    body, aliases = _proj_kernel, {}
    if stacks is not None:
        args = (*stacks, *args)
        in_specs = [pl.BlockSpec(memory_space=pl.ANY)] * 2 + in_specs
        body, aliases = _proj_kernel_stacked, {0: 0, 1: 1}
    return pl.pallas_call(
        functools.partial(body, tm=tm),
        grid=(b, nt),
        in_specs=in_specs,
        out_specs=[o[1] for o in outs],
        out_shape=[o[0] for o in outs],
        input_output_aliases=aliases,
        scratch_shapes=[pltpu.VMEM((1, CARRY_ROWS, LRU_WIDTH), F32), pltpu.VMEM((1, LRU_WIDTH), F32),
                        pltpu.VMEM((tm, LRU_WIDTH), F32)],
        compiler_params=_cparams(("arbitrary", "arbitrary")),
        name="proj",
    )(*args)


def _attn_lambda(lq_ref, lam_init):
    lq = lq_ref[...]
    s1 = jnp.sum(lq[0:1] * lq[1:2], axis=1, keepdims=True)
    s2 = jnp.sum(lq[2:3] * lq[3:4], axis=1, keepdims=True)
    return jnp.exp(s1) - jnp.exp(s2) + lam_init


def _attn_norm(o, sub, lam_init):
    ms = jnp.mean(o * o, axis=-1, keepdims=True)
    return (o * lax.rsqrt(ms + EPS) * sub * (1.0 - lam_init)).astype(BF16)


def _chunk_mask(q0, k0, tq, tk, keys_on_rows=False):
    shape = (tk, tq) if keys_on_rows else (tq, tk)
    qpos = q0 + lax.broadcasted_iota(jnp.int32, shape, 1 if keys_on_rows else 0)
    kpos = k0 + lax.broadcasted_iota(jnp.int32, shape, 0 if keys_on_rows else 1)
    return (kpos // CHUNK) <= (qpos // CHUNK)


def _attn_prompt_kernel(lq_ref, sub_ref, q_ref, k_ref, vt_ref, o_ref, m_sc, acc_sc, s_sc, mt_sc,
                        *, tq, tk, tv, lam_init):
    i = pl.program_id(2)
    m_sc[...] = jnp.full(m_sc.shape, NEG_BIG, F32)
    acc_sc[...] = jnp.zeros(acc_sc.shape, F32)
    q = q_ref[0]
    qs = (q[:, :LANES], q[:, LANES:])
    ones = jnp.ones((BF16_SUBLANES, tk), BF16)

    diag_visible = _chunk_mask(0, 0, tk, tk, keys_on_rows=True)

    def scores(t, slot, diagonal, lo=0):
        k = k_ref[0, pl.ds(pl.multiple_of(t * tk, tk), tk), :]
        for c in range(2):
            s = _dot_nt(k, qs[c][lo:])
            if diagonal:
                lead = jnp.where(diag_visible, s[:, :tk], NEG_BIG)
                s = lead if tq - lo == tk else jnp.concatenate([lead, s[:, tk:]], axis=1)
            s_sc[slot, c, :, lo:] = s
            mt_sc[slot, c, :, lo:] = jnp.max(s, axis=0, keepdims=True)

    def consume(t, slot, lo=0):
        vt = [vt_ref[0, 0, t * (tk // tv) + u] for u in range(tk // tv)]
        vt_aug = jnp.concatenate([_lane_cat(vt), ones], axis=0)
        for c in range(2):
            m_prev = m_sc[c, :, lo:]
            m_new = jnp.maximum(m_prev, mt_sc[slot, c, :, lo:])
            alpha = jnp.exp2(m_prev - m_new)
            p = jnp.exp2(s_sc[slot, c, :, lo:] - m_new).astype(BF16)
            acc_sc[c, :, lo:] = acc_sc[c, :, lo:] * alpha + _dot(vt_aug, p)
            m_sc[c, :, lo:] = m_new

    n_diag = tq // tk
    n_full = i * n_diag
    n_plain = jnp.maximum(n_full - 1, 0)

    @pl.when(i == 0)
    def _():
        scores(0, 0, True)

    @pl.when(i > 0)
    def _():
        scores(0, 0, False)

    def pair(p, carry):
        t = 2 * p
        scores(t + 1, 1, False)
        consume(t, 0)
        scores(t + 2, 0, False)
        consume(t + 1, 1)
        return carry

    lax.fori_loop(0, n_plain // 2, pair, 0)
    tc = (n_plain // 2) * 2

    def tail(n_unmasked, n_masked):
        count = n_unmasked + n_masked
        lo = [max(u - n_unmasked, 0) * tk for u in range(count)]
        for u in range(count - 1):
            scores(tc + u + 1, (u + 1) % 2, u + 1 >= n_unmasked, lo[u + 1])
            consume(tc + u, u % 2, lo[u])
        consume(tc + count - 1, (count - 1) % 2, lo[count - 1])

    @pl.when(n_full == 0)
    def _():
        tail(0, n_diag)

    @pl.when(jnp.logical_and(n_full > 0, n_plain % 2 == 0))
    def _():
        tail(1, n_diag)

    @pl.when(jnp.logical_and(n_full > 0, n_plain % 2 == 1))
    def _():
        tail(2, n_diag)

    a0, a1 = acc_sc[0], acc_sc[1]
    lam = _attn_lambda(lq_ref, lam_init)
    o_t = a0[:A_VDIM] / a0[A_VDIM:A_VDIM + 1] - lam * (a1[:A_VDIM] / a1[A_VDIM:A_VDIM + 1])
    o_ref[0] = _attn_norm(o_t.T, sub_ref[...], lam_init)


def _lane_cat(parts):
    return parts[0] if len(parts) == 1 else jnp.concatenate(parts, axis=1)


def _attn_prompt_call(lq, sub, qb, kb, vt, lam_init):
    b, t, _ = kb.shape
    tv = vt.shape[-1]
    tq = min(t, 2048)
    tk = min(t, 512)
    assert t % tq == 0 and (tq % CHUNK == 0 or tq == t) and tq % LANES == 0 and tk % tv == 0 and tq % tk == 0
    return pl.pallas_call(
        functools.partial(_attn_prompt_kernel, tq=tq, tk=tk, tv=tv, lam_init=lam_init),
        grid=(b, A_HEADS, t // tq),
        in_specs=[pl.BlockSpec(lq.shape, lambda bi, h, i: (0, 0)),
                  pl.BlockSpec(sub.shape, lambda bi, h, i: (0, 0)),
                  pl.BlockSpec((1, tq, 2 * LANES), lambda bi, h, i: (bi, i, h)),
                  pl.BlockSpec((1, t, LANES), lambda bi, h, i: (bi, 0, h)),
                  pl.BlockSpec((1, 1, t // tv, A_VDIM, tv), lambda bi, h, i: (bi, h, 0, 0, 0))],
        out_specs=pl.BlockSpec((1, tq, LANES), lambda bi, h, i: (bi, i, h)),
        out_shape=jax.ShapeDtypeStruct((b, t, A_WIDTH), BF16),
        scratch_shapes=[pltpu.VMEM((2, 1, tq), F32), pltpu.VMEM((2, A_VDIM + BF16_SUBLANES, tq), F32),
                        pltpu.VMEM((2, 2, tk, tq), F32), pltpu.VMEM((2, 2, 1, tq), F32)],
        compiler_params=_cparams(("arbitrary", "arbitrary", "arbitrary")),
        name="attn_prompt",
    )(lq, sub, qb, kb, vt)


def _attn_cached_kernel(lq_ref, sub_ref, q_ref, kpt_ref, vp_ref, kn_ref, vn_ref, o_ref, *, t, p_len, lam_init):
    mask_p = jnp.concatenate([_chunk_mask(p_len, 0, t, p_len)] * 2, axis=0)
    mask_n = jnp.concatenate([_chunk_mask(p_len, p_len, t, t)] * 2, axis=0)
    lam = _attn_lambda(lq_ref, lam_init)
    ones_p = jnp.ones((p_len, LANES), BF16)
    ones_n = jnp.ones((t, LANES), BF16)
    for h in range(A_HEADS):
        cols = slice(h * LANES, (h + 1) * LANES)
        kpt = kpt_ref[0, 0, cols, :].astype(BF16)
        kn = kn_ref[0, :, cols]
        vp = jnp.concatenate([vp_ref[0, 0, pl.ds(h, p_len, stride=A_HEADS), :].astype(BF16), ones_p], axis=1)
        vn = jnp.concatenate([vn_ref[0, 0, pl.ds(h, t, stride=A_HEADS), :].astype(BF16), ones_n], axis=1)
        q2 = jnp.concatenate([q_ref[0, :, 2 * h * LANES:(2 * h + 1) * LANES],
                              q_ref[0, :, (2 * h + 1) * LANES:(2 * h + 2) * LANES]], axis=0)
        s_p = jnp.where(mask_p, _dot(q2, kpt), NEG_BIG)
        s_n = jnp.where(mask_n, _dot_nt(q2, kn), NEG_BIG)
        m = jnp.maximum(jnp.max(s_p, axis=1, keepdims=True), jnp.max(s_n, axis=1, keepdims=True))
        acc = _dot(jnp.exp2(s_p - m).astype(BF16), vp) + _dot(jnp.exp2(s_n - m).astype(BF16), vn)
        ratio = acc[:, :LANES] / acc[:, LANES:]
        o = ratio[:t] - lam * ratio[t:]
        o_ref[0, :, cols] = _attn_norm(o, sub_ref[...], lam_init)


def _attn_cached_call(lq, sub, qb, k_new, v_new_rows, k_past_t, v_past_rows, layer, lam_init):
    b, t, _ = k_new.shape
    p_len = k_past_t.shape[-1]
    aw = 2 * A_HEADS * A_HEAD_DIM
    return pl.pallas_call(
        functools.partial(_attn_cached_kernel, t=t, p_len=p_len, lam_init=lam_init),
        grid=(b,),
        in_specs=[pl.BlockSpec(lq.shape, lambda bi: (0, 0)),
                  pl.BlockSpec(sub.shape, lambda bi: (0, 0)),
                  pl.BlockSpec((1, t, 2 * aw), lambda bi: (bi, 0, 0)),
                  pl.BlockSpec((1, 1, aw, p_len), lambda bi: (layer, bi, 0, 0)),
                  pl.BlockSpec((1, 1, p_len * A_HEADS, A_VDIM), lambda bi: (layer, bi, 0, 0)),
                  pl.BlockSpec((1, t, aw), lambda bi: (bi, 0, 0)),
                  pl.BlockSpec((1, 1, t * A_HEADS, A_VDIM), lambda bi: (layer, bi, 0, 0))],
        out_specs=pl.BlockSpec((1, t, A_WIDTH), lambda bi: (bi, 0, 0)),
        out_shape=jax.ShapeDtypeStruct((b, t, A_WIDTH), BF16),
        compiler_params=_cparams(("arbitrary",)),
        name="attn_cached",
    )(lq, sub, qb, k_past_t, v_past_rows, k_new, v_new_rows)


def _split3(x):
    hi = x.astype(BF16)
    r1 = x - hi.astype(F32)
    mid = r1.astype(BF16)
    lo = (r1 - mid.astype(F32)).astype(BF16)
    return hi, mid, lo


def _gla_kernel(q_ref, k_ref, v_ref, r_ref, la_ref, gn_ref, st0_ref, o_ref, sto_ref, st, *, tg, lc):
    t = pl.program_id(1)

    @pl.when(t == 0)
    def _():
        st[...] = st0_ref[0]

    rr = lax.broadcasted_iota(jnp.int32, (lc, lc), 0)
    cc = lax.broadcasted_iota(jnp.int32, (lc, lc), 1)
    tri = rr >= cc
    tri_b = jnp.where(tri, 1.0, 0.0).astype(BF16)
    lane_k = lax.broadcasted_iota(jnp.int32, (lc, G_QK), 1) // G_DK
    bd = (lax.broadcasted_iota(jnp.int32, (G_WIDTH, G_QK), 0) // G_DV
          == lax.broadcasted_iota(jnp.int32, (G_WIDTH, G_QK), 1) // G_DK)
    gn = gn_ref[...]

    per_chunk = []
    for ci in range(tg // lc):
        rows = slice(ci * lc, (ci + 1) * lc)
        hi, mid, lo = _split3(la_ref[0, rows, :])
        bc = _dot(tri_b, hi) + _dot(tri_b, mid) + _dot(tri_b, lo)
        b_mid = bc[lc // 2 - 1:lc // 2]
        b_last = bc[lc - 1:lc]
        q = q_ref[0, rows, :]
        k = k_ref[0, rows, :]
        vb = v_ref[0, rows, :]
        q_in = (q * jnp.exp(bc)).astype(BF16)
        q_mid = q * jnp.exp(bc - b_mid)
        k_mid = (k * jnp.exp(b_mid - bc)).astype(BF16)
        k_end = (k * jnp.exp(b_last - bc)).astype(BF16)
        att = []
        for h in range(G_HEADS):
            qh = jnp.where(lane_k == h, q_mid, 0.0).astype(BF16)
            att.append(jnp.where(tri, _dot_nt(qh, k_mid), 0.0).astype(BF16))
        incr = jnp.where(bd, _dot_tn(vb, k_end), 0.0)
        per_chunk.append((rows, q_in, att, vb, incr, jnp.exp(b_last)))

    intras = [[_dot(att[h], vb[:, h * G_DV:(h + 1) * G_DV]) for h in range(G_HEADS)]
              for _, _, att, vb, _, _ in per_chunk]

    s_cur = st[...]
    for (rows, q_in, _, _, incr, decay), intra in zip(per_chunk, intras):
        inter = _dot_nt(q_in, s_cur.astype(BF16))
        outs = []
        for h in range(G_HEADS):
            oh = inter[:, h * G_DV:(h + 1) * G_DV] + intra[h]
            ms = jnp.mean(oh * oh, axis=-1, keepdims=True)
            outs.append(oh * lax.rsqrt(ms + EPS) * gn)
        gr = r_ref[0, rows, :]
        o_ref[0, rows, :] = (jnp.concatenate(outs, axis=1) * (gr * _sigmoid(gr))).astype(BF16)
        s_cur = s_cur * decay + incr
    st[...] = s_cur
    sto_ref[0] = s_cur


def _gla_call(gq, gk, gv, gr, la, gn, st0):
    b, t, _ = gq.shape
    lc = min(CHUNK, t)
    tg = min(t, 1024)
    assert t % tg == 0 and tg % lc == 0 and lc % 16 == 0
    row = lambda bi, ti: (bi, ti, 0)
    st_spec = pl.BlockSpec((1, G_WIDTH, G_QK), lambda bi, ti: (bi, 0, 0))
    return pl.pallas_call(
        functools.partial(_gla_kernel, tg=tg, lc=lc),
        grid=(b, t // tg),
        in_specs=[pl.BlockSpec((1, tg, G_QK), row), pl.BlockSpec((1, tg, G_QK), row),
                  pl.BlockSpec((1, tg, G_WIDTH), row), pl.BlockSpec((1, tg, G_WIDTH), row),
                  pl.BlockSpec((1, tg, G_QK), row), pl.BlockSpec(gn.shape, lambda bi, ti: (0, 0)), st_spec],
        out_specs=[pl.BlockSpec((1, tg, G_WIDTH), row), st_spec],
        out_shape=[jax.ShapeDtypeStruct((b, t, G_WIDTH), BF16), jax.ShapeDtypeStruct((b, G_WIDTH, G_QK), F32)],
        scratch_shapes=[pltpu.VMEM((G_WIDTH, G_QK), F32)],
        compiler_params=_cparams(("arbitrary", "arbitrary")),
        name="gla",
    )(gq, gk, gv, gr, la, gn, st0)


def _merge_kernel(x_ref, ao_ref, go_ref, lo_ref, nm_ref, wm_ref, bm_ref, wba_ref, wbg_ref, wbl_ref, wo_ref, o_ref):
    x = x_ref[...]
    xb = _rms(x, nm_ref[...]).astype(BF16)
    g = _sigmoid(_dot(xb, wm_ref[...]) + bm_ref[...])
    merged = (g[:, :D_MODEL] * _dot(ao_ref[...], wba_ref[...])
              + g[:, D_MODEL:2 * D_MODEL] * _dot(go_ref[...], wbg_ref[...])
              + g[:, 2 * D_MODEL:] * _dot(lo_ref[...], wbl_ref[...]))
    o_ref[...] = x + _dot(merged.astype(BF16), wo_ref[...])


def _merge_call(x2, ao, go, lo, w):
    n, d = x2.shape
    tm = min(n, 512)
    assert n % tm == 0
    row = lambda i: (i, 0)
    weights = (w['norm_mix'], w['w_merge'], w['b_merge'], w['w_branch_attn'], w['w_branch_gla'], w['w_branch_lru'],
               w['w_out'])
    wspecs = [pl.BlockSpec(a.shape, lambda i: (0, 0), pipeline_mode=pl.Buffered(1)) for a in weights]
    return pl.pallas_call(
        _merge_kernel,
        grid=(n // tm,),
        in_specs=[pl.BlockSpec((tm, d), row), pl.BlockSpec((tm, A_WIDTH), row), pl.BlockSpec((tm, G_WIDTH), row),
                  pl.BlockSpec((tm, LRU_WIDTH), row)] + wspecs,
        out_specs=pl.BlockSpec((tm, d), row),
        out_shape=jax.ShapeDtypeStruct((n, d), F32),
        compiler_params=_cparams(("arbitrary",)),
        name="merge",
    )(x2, ao, go, lo, *weights)


def _ffn_kernel(x_ref, nf_ref, wg_ref, cw_ref, cb_ref, wu_ref, wd_ref, nl_ref, buf0_ref, o_ref, bufo_ref, ubuf,
                *, seqs, tm, final):
    t = pl.program_id(1)
    x = x_ref[...].reshape(seqs * tm, D_MODEL)
    hb = _rms(x, nf_ref[...]).astype(BF16)

    @pl.when(t == 0)
    def _():
        ubuf[...] = buf0_ref[...]

    gu = _dot(hb, wg_ref[...])
    gc = _causal_conv(gu, ubuf[...], cw_ref[...], cb_ref[...])
    tail = _last_groups(gu, seqs)
    bufo_ref[...] = tail
    ubuf[...] = tail
    f = _gelu_tanh(gc) * _dot(hb, wu_ref[...])
    y = x + _dot(f.astype(BF16), wd_ref[...])
    y = _rms(y, nl_ref[...]) if final else y
    o_ref[...] = y.reshape(seqs, tm, D_MODEL)


def _ffn_call(x, w, norm_last, buf0, final):
    b, t, d = x.shape
    tm = min(t, 512)
    assert t % tm == 0
    seqs = max(1, min(b, 256 // tm)) if tm == t else 1
    assert b % seqs == 0
    row = lambda bi, ti: (bi, ti, 0)
    const2 = lambda bi, ti: (0, 0)
    buf_spec = pl.BlockSpec((seqs, CARRY_ROWS, D_FF), lambda bi, ti: (bi, 0, 0))
    weights = (w['norm_ffn'], w['w_ffn_gate'], w['ffn_conv_w'], w['ffn_conv_b'], w['w_ffn_up'], w['w_ffn_down'],
               norm_last)
    wspecs = [pl.BlockSpec(a.shape, const2, pipeline_mode=pl.Buffered(1)) for a in weights]
    return pl.pallas_call(
        functools.partial(_ffn_kernel, seqs=seqs, tm=tm, final=final),
        grid=(b // seqs, t // tm),
        in_specs=[pl.BlockSpec((seqs, tm, d), row)] + wspecs + [buf_spec],
        out_specs=[pl.BlockSpec((seqs, tm, d), row), buf_spec],
        out_shape=[jax.ShapeDtypeStruct((b, t, d), F32), jax.ShapeDtypeStruct((b, CARRY_ROWS, D_FF), F32)],
        scratch_shapes=[pltpu.VMEM((seqs, CARRY_ROWS, D_FF), F32)],
        compiler_params=_cparams(("arbitrary", "arbitrary")),
        name="ffn",
    )(x, *weights, buf0)


def _rope_tables(p_len, t):
    half = ROPE_DIM // 2
    pos = (p_len + jnp.arange(t, dtype=jnp.int32)).astype(F32)
    inv = ROPE_THETA ** (-jnp.arange(half, dtype=F32) / half)
    ang = pos[:, None] * inv[None, :]
    cos, sin = jnp.cos(ang), jnp.sin(ang)
    rest = A_HEAD_DIM - ROPE_DIM
    z = jnp.zeros((t, half), F32)
    c64 = jnp.concatenate([cos, cos, jnp.ones((t, rest), F32)], axis=1)
    sa64 = jnp.concatenate([-sin, z, jnp.zeros((t, rest), F32)], axis=1)
    sb64 = jnp.concatenate([z, sin, jnp.zeros((t, rest), F32)], axis=1)
    rep = LANES // A_HEAD_DIM
    return tuple(jnp.tile(a, (1, rep)) for a in (c64, sa64, sb64))


def _block_diag(wb):
    n, bi, bo = wb.shape
    eye = jnp.eye(n, dtype=wb.dtype)
    return (eye[:, None, :, None] * wb[:, :, None, :]).reshape(n * bi, n * bo)


def _prep_layer(l, p):
    aw = 2 * A_HEADS * A_HEAD_DIM
    o_gla = 2 * aw + A_WIDTH
    o_ga = o_gla + 2 * G_QK + 2 * G_WIDTH
    o_lru = o_ga + G_GATE_RANK
    w_in = p['w_in'][l]
    row = lambda a: a[l].reshape(1, -1)
    return {
        'norm_mix': row(p['norm_mix']),
        'wqkv': w_in[:, :o_gla].astype(BF16),
        'wgla': w_in[:, o_gla:o_ga].astype(BF16),
        'wga': w_in[:, o_ga:o_lru].astype(BF16),
        'wlru': w_in[:, o_lru:].astype(BF16),
        'wg2': p['w_gla_gate2'][l].astype(BF16),
        'bg2': row(p['b_gla_gate']),
        'lambda_qk': p['lambda_qk'][l],
        'attn_subln': row(p['attn_subln']),
        'gla_norm': row(p['gla_norm']),
        'lru_conv_w': p['lru_conv_w'][l],
        'lru_conv_b': row(p['lru_conv_b']),
        'wa_bd': _block_diag(p['lru_wa'][l]).astype(BF16),
        'lru_ba': row(p['lru_ba']),
        'wx_bd': _block_diag(p['lru_wx'][l]).astype(BF16),
        'lru_bx': row(p['lru_bx']),
        'lru_lambda': row(p['lru_lambda']),
        'w_branch_attn': p['w_branch_attn'][l].astype(BF16),
        'w_branch_gla': p['w_branch_gla'][l].astype(BF16),
        'w_branch_lru': p['w_branch_lru'][l].astype(BF16),
        'w_merge': p['w_merge'][l].astype(BF16),
        'b_merge': row(p['b_merge']),
        'w_out': p['w_out'][l].astype(BF16),
        'norm_ffn': row(p['norm_ffn']),
        'w_ffn_gate': p['w_ffn_gate'][l].astype(BF16),
        'ffn_conv_w': p['ffn_conv_w'][l],
        'ffn_conv_b': row(p['ffn_conv_b']),
        'w_ffn_up': p['w_ffn_up'][l].astype(BF16),
        'w_ffn_down': p['w_ffn_down'][l].astype(BF16),
    }


def _pad_carry(buf):
    return jnp.pad(buf, ((0, 0), (CARRY_ROWS - buf.shape[1], 0), (0, 0)))


def _state_to_kernel(s):
    b = s.shape[0]
    eye = jnp.eye(G_HEADS, dtype=s.dtype)
    st = jnp.swapaxes(s, 2, 3)
    return (st[:, :, :, None, :] * eye[None, :, None, :, None]).reshape(b, G_WIDTH, G_QK)


def _state_from_kernel(st):
    b = st.shape[0]
    s5 = st.reshape(b, G_HEADS, G_DV, G_HEADS, G_DK)
    diag = jnp.stack([s5[:, h, :, h, :] for h in range(G_HEADS)], axis=1)
    return jnp.swapaxes(diag, 2, 3)


def _trunk(x, caches, layers, norm_final):
    cache_k, cache_v, st_gla, st_lconv, st_lh, st_fconv = caches
    b, t, d = x.shape
    p_len = 0 if cache_k is None else cache_k.shape[2]
    tabs = _rope_tables(p_len, t)
    if cache_k is not None:
        depth = cache_k.shape[0]
        cache_kt = jnp.transpose(cache_k, (0, 1, 3, 4, 5, 2)).reshape(depth, b, -1, p_len)
        cache_vr = cache_v.reshape(depth, b, p_len * A_HEADS, A_VDIM)
    outs = [[], [], [], [], [], []]
    stacks = None
    for l, w in enumerate(layers):
        lam_init = 0.8 - 0.6 * math.exp(-0.3 * l)
        (kt_all, vr_all, qb, kb, vt, gq, gk, gv, gr, la, lo, lbuf, hl) = _proj_call(
            x, tabs, w, _pad_carry(st_lconv[l]), st_lh[l].reshape(b, 1, LRU_WIDTH), l, len(layers), stacks)
        stacks = (kt_all, vr_all)
        if cache_k is None:
            ao = _attn_prompt_call(w['lambda_qk'], w['attn_subln'], qb, kb, vt, lam_init)
        else:
            ao = _attn_cached_call(w['lambda_qk'], w['attn_subln'], qb, kb, vr_all, cache_kt, cache_vr, l, lam_init)
        go, st_new = _gla_call(gq, gk, gv, gr, la, w['gla_norm'], _state_to_kernel(st_gla[l]))
        x1 = _merge_call(x.reshape(b * t, d), ao.reshape(b * t, -1), go.reshape(b * t, -1), lo.reshape(b * t, -1), w)
        x, fbuf = _ffn_call(x1.reshape(b, t, d), w, norm_final.reshape(1, -1), _pad_carry(st_fconv[l]),
                            final=(l == len(layers) - 1))
        outs[2].append(_state_from_kernel(st_new))
        outs[3].append(lbuf[:, CARRY_ROWS - (LRU_CONV - 1):])
        outs[4].append(hl.reshape(b, LRU_WIDTH))
        outs[5].append(fbuf[:, CARRY_ROWS - (FFN_CONV - 1):])
    kt_all, vr_all = stacks
    depth = len(layers)
    new_k = jnp.transpose(kt_all.reshape(depth, b, A_HEADS, 2, A_HEAD_DIM, t), (0, 1, 5, 2, 3, 4))
    new_v = vr_all.reshape(depth, b, t, A_HEADS, A_VDIM)
    return x, [new_k, new_v] + [jnp.stack(o) for o in outs[2:]]


@jax.jit
def _forward(x_prompt, x_sample, cache_attn_k, cache_attn_v, state_gla, state_lru_conv, state_lru_h, state_ffn_conv,
             params, norm_final):
    layers = [_prep_layer(l, params) for l in range(DEPTH)]
    bp = x_prompt.shape[0]
    zeros = (None, None,
             jnp.zeros((DEPTH, bp, G_HEADS, G_DK, G_DV), F32),
             jnp.zeros((DEPTH, bp, LRU_CONV - 1, LRU_WIDTH), F32),
             jnp.zeros((DEPTH, bp, LRU_WIDTH), F32),
             jnp.zeros((DEPTH, bp, FFN_CONV - 1, D_FF), F32))
    y_p, new_p = _trunk(x_prompt, zeros, layers, norm_final)
    y_s, new_s = _trunk(x_sample, (cache_attn_k, cache_attn_v, state_gla, state_lru_conv, state_lru_h,
                                   state_ffn_conv), layers, norm_final)
    return (y_p, y_s, *new_p, *new_s)


def kernel(x_prompt, x_sample, cache_attn_k, cache_attn_v, state_gla, state_lru_conv, state_lru_h, state_ffn_conv, norm_mix, w_in, lambda_qk, attn_subln, w_gla_gate2, b_gla_gate, gla_norm, lru_conv_w, lru_conv_b, lru_wa, lru_ba, lru_wx, lru_bx, lru_lambda, w_branch_attn, w_branch_gla, w_branch_lru, w_merge, b_merge, w_out, norm_ffn, w_ffn_gate, ffn_conv_w, ffn_conv_b, w_ffn_up, w_ffn_down, norm_final):
    params = dict(norm_mix=norm_mix, w_in=w_in, lambda_qk=lambda_qk, attn_subln=attn_subln, w_gla_gate2=w_gla_gate2,
                  b_gla_gate=b_gla_gate, gla_norm=gla_norm, lru_conv_w=lru_conv_w, lru_conv_b=lru_conv_b,
                  lru_wa=lru_wa, lru_ba=lru_ba, lru_wx=lru_wx, lru_bx=lru_bx, lru_lambda=lru_lambda,
                  w_branch_attn=w_branch_attn, w_branch_gla=w_branch_gla, w_branch_lru=w_branch_lru,
                  w_merge=w_merge, b_merge=b_merge, w_out=w_out, norm_ffn=norm_ffn, w_ffn_gate=w_ffn_gate,
                  ffn_conv_w=ffn_conv_w, ffn_conv_b=ffn_conv_b, w_ffn_up=w_ffn_up, w_ffn_down=w_ffn_down)
    return _forward(x_prompt, x_sample, cache_attn_k, cache_attn_v, state_gla, state_lru_conv, state_lru_h,
                    state_ffn_conv, params, norm_final)
```
